```python
import jax, jax.numpy as jnp
from jax import lax
import numpy as np

D_MODEL = 2048
BATCH = 8
SEQ = 2048
DEPTH = 1

GM_WIDTH = 2048
CHUNK = 128
GM_GROUPS = 16
GM_GROUP_DIM = GM_WIDTH // GM_GROUPS
MLA_HEADS = 16
Q_LORA = 512
KV_LORA = 256
QK_NOPE = 128
QK_ROPE = 64
V_HEAD = 128
ROPE_THETA = 10000.0
Q_BLOCK = 128
D_FF = 5632
CONV_W = 3
EPS = 1e-6
N_MOD = 6
IN_SIZES = (GM_WIDTH, GM_WIDTH, Q_LORA, KV_LORA, QK_ROPE, D_MODEL, D_MODEL)
IN_COLS = sum(IN_SIZES)
IN_SPLITS = tuple(int(s) for s in np.cumsum(IN_SIZES)[:-1])

kernel_name = "hybrid_gmlp_mla_convffn_block"


def rmsnorm(x, g):
    xf = x.astype(jnp.float32)
    y = xf * lax.rsqrt(jnp.mean(xf * xf, axis=-1, keepdims=True) + EPS)
    return (y * g.astype(jnp.float32)).astype(x.dtype)


def layernorm(x, g, b):
    xf = x.astype(jnp.float32)
    mu = jnp.mean(xf, axis=-1, keepdims=True)
    var = jnp.mean(jnp.square(xf - mu), axis=-1, keepdims=True)
    y = (xf - mu) * lax.rsqrt(var + EPS)
    return (y * g.astype(jnp.float32) + b.astype(jnp.float32)).astype(x.dtype)


def rope_tables(positions, dtype):
    inv = ROPE_THETA ** (-jnp.arange(0, QK_ROPE, 2, dtype=jnp.float32) / QK_ROPE)
    ang = positions.astype(jnp.float32)[..., None] * inv
    return jnp.cos(ang).astype(dtype), jnp.sin(ang).astype(dtype)


def apply_rope(x, cos, sin):
    x1, x2 = jnp.split(x, 2, axis=-1)
    return jnp.concatenate([x1 * cos - x2 * sin, x2 * cos + x1 * sin], axis=-1)


def gmlp_spatial_gating(u, v, ln_g, ln_b, w_s, b_s):
    B, S, _ = v.shape
    v = layernorm(v, ln_g, ln_b)
    v = v.reshape(B, S // CHUNK, CHUNK, GM_GROUPS, GM_GROUP_DIM)
    mask = jnp.tril(jnp.ones((CHUNK, CHUNK), dtype=w_s.dtype))
    mixed = jnp.einsum('bnpgd,gqp->bnqgd', v, w_s * mask) + b_s.T[None, None, :, :, None]
    return u * mixed.reshape(B, S, GM_WIDTH)


def mla_attention(q_lat, kv_lat, k_pe, positions, q_norm_g, w_uq, kv_norm_g, w_ukv):
    B, S, _ = q_lat.shape
    q = (rmsnorm(q_lat, q_norm_g) @ w_uq).reshape(B, S, MLA_HEADS, QK_NOPE + QK_ROPE)
    kv = (rmsnorm(kv_lat, kv_norm_g) @ w_ukv).reshape(B, S, MLA_HEADS, QK_NOPE + V_HEAD)
    q_nope, q_pe = q[..., :QK_NOPE], q[..., QK_NOPE:]
    k_nope, v = kv[..., :QK_NOPE], kv[..., QK_NOPE:]
    cos, sin = rope_tables(positions, q.dtype)
    q_pe = apply_rope(q_pe, cos[:, :, None], sin[:, :, None])
    k_pe = apply_rope(k_pe, cos, sin)
    q = jnp.concatenate([q_nope, q_pe], axis=-1)
    k = jnp.concatenate([k_nope, jnp.broadcast_to(k_pe[:, :, None], (B, S, MLA_HEADS, QK_ROPE))], axis=-1)
    scale = (QK_NOPE + QK_ROPE) ** -0.5
    n_blocks = S // Q_BLOCK
    q_blocks = q.reshape(B, n_blocks, Q_BLOCK, MLA_HEADS, QK_NOPE + QK_ROPE).transpose(1, 0, 2, 3, 4)
    key_pos = jnp.arange(S)

    def attend(args):
        qb, i = args
        s = jnp.einsum('bqhd,bkhd->bhqk', qb, k).astype(jnp.float32) * scale
        q_pos = i * Q_BLOCK + jnp.arange(Q_BLOCK)
        causal = key_pos[None, :] <= q_pos[:, None]
        s = jnp.where(causal[None, None], s, -1e30)
        p = jax.nn.softmax(s, axis=-1).astype(v.dtype)
        return jnp.einsum('bhqk,bkhd->bqhd', p, v)

    o = lax.map(attend, (q_blocks, jnp.arange(n_blocks)))
    return o.transpose(1, 0, 2, 3, 4).reshape(B, S, MLA_HEADS * V_HEAD)


def causal_dwconv(h, w, b):
    S = h.shape[1]
    hp = jnp.pad(h, ((0, 0), (CONV_W - 1, 0), (0, 0)))
    return sum(w[k] * hp[:, k:k + S] for k in range(CONV_W)) + b


def setup_inputs(seed: int = 0) -> dict:
    key = jax.random.key(seed)
    ks = jax.random.split(key, 32)
    f32 = jnp.float32
    nrm = lambda k, shape, s: jax.random.normal(k, shape, f32) * s
    gain = lambda k, n: 1.0 + 0.02 * jax.random.normal(k, (n,), f32)
    offset = jax.random.randint(ks[2], (BATCH, 1), 0, 4096, dtype=jnp.int32)
    positions = (jnp.arange(SEQ, dtype=jnp.int32)[None, :] + offset).astype(jnp.int32)
    return {
        "x": nrm(ks[0], (BATCH, SEQ, D_MODEL), 1.0),
        "c": nrm(ks[1], (BATCH, D_MODEL), 1.0),
        "positions": positions,
        "w_ada": nrm(ks[3], (D_MODEL, N_MOD * D_MODEL), 0.5 * D_MODEL ** -0.5),
        "b_ada": nrm(ks[4], (N_MOD * D_MODEL,), 0.01),
        "pre_norm1_g": gain(ks[5], D_MODEL),
        "w_in": nrm(ks[6], (D_MODEL, IN_COLS), D_MODEL ** -0.5),
        "gm_ln_g": gain(ks[7], GM_WIDTH),
        "gm_ln_b": nrm(ks[8], (GM_WIDTH,), 0.01),
        "gm_w_s": nrm(ks[9], (GM_GROUPS, CHUNK, CHUNK), CHUNK ** -0.5),
        "gm_b_s": 1.0 + 0.02 * jax.random.normal(ks[10], (GM_GROUPS, CHUNK), f32),
        "w_branch_a": nrm(ks[11], (GM_WIDTH, D_MODEL), GM_WIDTH ** -0.5),
        "q_norm_g": gain(ks[12], Q_LORA),
        "w_uq": nrm(ks[13], (Q_LORA, MLA_HEADS * (QK_NOPE + QK_ROPE)), Q_LORA ** -0.5),
        "kv_norm_g": gain(ks[14], KV_LORA),
        "w_ukv": nrm(ks[15], (KV_LORA, MLA_HEADS * (QK_NOPE + V_HEAD)), KV_LORA ** -0.5),
        "w_branch_b": nrm(ks[16], (MLA_HEADS * V_HEAD, D_MODEL), (MLA_HEADS * V_HEAD) ** -0.5),
        "w_out": nrm(ks[17], (D_MODEL, D_MODEL), D_MODEL ** -0.5),
        "post_norm1_g": gain(ks[18], D_MODEL),
        "pre_norm2_g": gain(ks[19], D_MODEL),
        "w_up": nrm(ks[20], (D_MODEL, 2 * D_FF), D_MODEL ** -0.5),
        "conv_w": nrm(ks[21], (CONV_W, 2 * D_FF), CONV_W ** -0.5),
        "conv_b": nrm(ks[22], (2 * D_FF,), 0.01),
        "w_down": nrm(ks[23], (D_FF, D_MODEL), D_FF ** -0.5),
        "post_norm2_g": gain(ks[24], D_MODEL),
    }


def reference(x, c, positions, w_ada, b_ada, pre_norm1_g, w_in, gm_ln_g, gm_ln_b, gm_w_s, gm_b_s,
              w_branch_a, q_norm_g, w_uq, kv_norm_g, w_ukv, w_branch_b, w_out, post_norm1_g,
              pre_norm2_g, w_up, conv_w, conv_b, w_down, post_norm2_g):
    B = x.shape[0]
    mod = (jax.nn.silu(c) @ w_ada + b_ada).reshape(B, N_MOD, D_MODEL)
    shift1, scale1, gate1 = mod[:, None, 0], mod[:, None, 1], mod[:, None, 2]
    shift2, scale2, gate2 = mod[:, None, 3], mod[:, None, 4], mod[:, None, 5]

    for _ in range(DEPTH):
        h = rmsnorm(x, pre_norm1_g) * (1.0 + scale1) + shift1
        z = h @ w_in
        u, v, q_lat, kv_lat, k_pe, g_a, g_b = jnp.split(z, IN_SPLITS, axis=-1)
        y_a = gmlp_spatial_gating(jax.nn.gelu(u), jax.nn.gelu(v), gm_ln_g, gm_ln_b, gm_w_s, gm_b_s) @ w_branch_a
        y_b = mla_attention(q_lat, kv_lat, k_pe, positions, q_norm_g, w_uq, kv_norm_g, w_ukv) @ w_branch_b
        merged = jax.nn.sigmoid(g_a) * y_a + jax.nn.sigmoid(g_b) * y_b
        x = x + gate1 * rmsnorm(merged @ w_out, post_norm1_g)

        h = rmsnorm(x, pre_norm2_g) * (1.0 + scale2) + shift2
        up = causal_dwconv(h @ w_up, conv_w, conv_b)
        gate_h, val_h = jnp.split(up, 2, axis=-1)
        ffn = (jax.nn.silu(gate_h) * val_h) @ w_down
        x = x + gate2 * rmsnorm(ffn, post_norm2_g)
    return x
```

```python
import functools

import jax
import jax.numpy as jnp
from jax import lax
from jax.experimental import pallas as pl
from jax.experimental.pallas import tpu as pltpu

F32 = jnp.float32
BF16 = jnp.bfloat16

D_MODEL = 2048
SEQ = 2048
GM_WIDTH = 2048
CHUNK = 128
GM_GROUPS = 16
MLA_HEADS = 16
Q_LORA = 512
KV_LORA = 256
QK_NOPE = 128
QK_ROPE = 64
V_HEAD = 128
ROPE_THETA = 10000.0
D_FF = 5632
CONV_W = 3
EPS = 1e-6
N_MOD = 6

LANES = 128
HEAD_PAD = 2 * LANES
HALO = 16
VMEM_LIMIT = 56 * 1024 * 1024

Z_TILE = 1024
Z_COLS = 9 * Z_TILE


def _rms(x, g):
    return x * lax.rsqrt(jnp.mean(x * x, axis=-1, keepdims=True) + EPS) * g


def _gelu_tanh(x):
    c = 0.7978845608028654
    return 0.5 * x * (1.0 + jnp.tanh(c * (x + 0.044715 * (x * x * x))))


def _rope_tables(pos_ref, inv_ref, sgn_ref):
    ang = pos_ref[...].astype(F32) * inv_ref[...]
    return jnp.cos(ang), jnp.sin(ang) * sgn_ref[...]


def _params(sem):
    return pltpu.CompilerParams(dimension_semantics=sem, vmem_limit_bytes=VMEM_LIMIT)


def _ada_kernel(c_ref, w_ref, b_ref, o_ref):
    c = c_ref[...]
    sc = (c * jax.nn.sigmoid(c)).astype(BF16)
    o_ref[...] = jnp.dot(sc, w_ref[...].astype(BF16), preferred_element_type=F32) + b_ref[...]


def _ada(c, w_ada, b_ada):
    bsz = c.shape[0]
    n = w_ada.shape[1]
    tn = 1024
    return pl.pallas_call(
        _ada_kernel,
        grid=(n // tn,),
        in_specs=[
            pl.BlockSpec((bsz, D_MODEL), lambda j: (0, 0)),
            pl.BlockSpec((D_MODEL, tn), lambda j: (0, j)),
            pl.BlockSpec((1, tn), lambda j: (0, j)),
        ],
        out_specs=pl.BlockSpec((bsz, tn), lambda j: (0, j)),
        out_shape=jax.ShapeDtypeStruct((bsz, n), F32),
        compiler_params=_params(("arbitrary",)),
        name="ada",
    )(c, w_ada, b_ada.reshape(1, n))


def _inproj_kernel(x_ref, mod_ref, g_ref, w_ref, pos_ref, inv_ref, sgn_ref, qg_ref, kvg_ref, z_ref, h_ref):
    j = pl.program_id(1)

    @pl.when(j == 0)
    def _():
        y = _rms(x_ref[...], g_ref[...])
        h_ref[...] = (y * (1.0 + mod_ref[0, 1:2, :]) + mod_ref[0, 0:1, :]).astype(BF16)

    acc = jnp.dot(h_ref[...], w_ref[...], preferred_element_type=F32)

    @pl.when(j < 4)
    def _():
        z_ref[...] = _gelu_tanh(acc).astype(BF16)

    @pl.when(jnp.logical_and(j >= 4, j < 8))
    def _():
        z_ref[...] = jax.nn.sigmoid(acc).astype(BF16)

    @pl.when(j == 8)
    def _():
        cos_t, sin_t = _rope_tables(pos_ref, inv_ref, sgn_ref)
        z_ref[:, 0:Q_LORA] = _rms(acc[:, 0:Q_LORA], qg_ref[...]).astype(BF16)
        z_ref[:, Q_LORA:Q_LORA + KV_LORA] = _rms(acc[:, Q_LORA:Q_LORA + KV_LORA], kvg_ref[...]).astype(BF16)
        o = Q_LORA + KV_LORA
        kr = acc[:, o:o + LANES] * cos_t + acc[:, o + LANES:o + 2 * LANES] * sin_t
        z_ref[:, o:o + LANES] = kr.astype(BF16)
        z_ref[:, o + LANES:o + 2 * LANES] = jnp.zeros((acc.shape[0], LANES), BF16)


def _inproj(x2d, mod, pre_g, w_z, pos2d, inv2, sgn2, q_g, kv_g, tm):
    t = x2d.shape[0]
    tiles_per_seq = SEQ // tm
    return pl.pallas_call(
        _inproj_kernel,
        grid=(t // tm, Z_COLS // Z_TILE),
        in_specs=[
            pl.BlockSpec((tm, D_MODEL), lambda i, j: (i, 0)),
            pl.BlockSpec((1, N_MOD, D_MODEL), lambda i, j: (i // tiles_per_seq, 0, 0)),
            pl.BlockSpec((1, D_MODEL), lambda i, j: (0, 0)),
            pl.BlockSpec((D_MODEL, Z_TILE), lambda i, j: (0, j)),
            pl.BlockSpec((tm, 1), lambda i, j: (i, 0)),
            pl.BlockSpec((1, LANES), lambda i, j: (0, 0)),
            pl.BlockSpec((1, LANES), lambda i, j: (0, 0)),
            pl.BlockSpec((1, Q_LORA), lambda i, j: (0, 0)),
            pl.BlockSpec((1, KV_LORA), lambda i, j: (0, 0)),
        ],
        out_specs=pl.BlockSpec((tm, Z_TILE), lambda i, j: (i, j)),
        out_shape=jax.ShapeDtypeStruct((t, Z_COLS), BF16),
        scratch_shapes=[pltpu.VMEM((tm, D_MODEL), BF16)],
        compiler_params=_params(("parallel", "arbitrary")),
        name="inproj",
    )(x2d, mod, pre_g, w_z, pos2d, inv2, sgn2, q_g, kv_g)


def _gmlp_kernel(u_ref, v_ref, ga_ref, lng_ref, lnb_ref, ws_ref, bs_ref, wa_ref, o_ref, vln_ref, a_ref):
    tm = u_ref.shape[0]
    v = v_ref[...].astype(F32)
    mu = jnp.mean(v, axis=-1, keepdims=True)
    var = jnp.mean(jnp.square(v - mu), axis=-1, keepdims=True)
    vln_ref[...] = ((v - mu) * lax.rsqrt(var + EPS) * lng_ref[...] + lnb_ref[...]).astype(BF16)
    row = lax.broadcasted_iota(jnp.int32, (CHUNK, CHUNK), 0)
    col = lax.broadcasted_iota(jnp.int32, (CHUNK, CHUNK), 1)
    mask = (col <= row).astype(F32)
    for g in range(GM_GROUPS):
        wsm = (ws_ref[g] * mask).astype(BF16)
        cs = slice(g * CHUNK, (g + 1) * CHUNK)
        for c in range(tm // CHUNK):
            rs = slice(c * CHUNK, (c + 1) * CHUNK)
            mixed = jnp.dot(wsm, vln_ref[rs, cs], preferred_element_type=F32) + bs_ref[g]
            a_ref[rs, cs] = (u_ref[rs, cs].astype(F32) * mixed).astype(BF16)
    ya = jnp.dot(a_ref[...], wa_ref[...], preferred_element_type=F32)
    o_ref[...] = (ga_ref[...].astype(F32) * ya).astype(BF16)


def _gmlp(z, ln_g, ln_b, w_s, b_s_bcast, w_a, tm):
    t = z.shape[0]
    return pl.pallas_call(
        _gmlp_kernel,
        grid=(t // tm,),
        in_specs=[
            pl.BlockSpec((tm, GM_WIDTH), lambda i: (i, 0)),
            pl.BlockSpec((tm, GM_WIDTH), lambda i: (i, 1)),
            pl.BlockSpec((tm, D_MODEL), lambda i: (i, 2)),
            pl.BlockSpec((1, GM_WIDTH), lambda i: (0, 0)),
            pl.BlockSpec((1, GM_WIDTH), lambda i: (0, 0)),
            pl.BlockSpec((GM_GROUPS, CHUNK, CHUNK), lambda i: (0, 0, 0)),
            pl.BlockSpec((GM_GROUPS, CHUNK, CHUNK), lambda i: (0, 0, 0)),
            pl.BlockSpec((GM_WIDTH, D_MODEL), lambda i: (0, 0)),
        ],
        out_specs=pl.BlockSpec((tm, D_MODEL), lambda i: (i, 0)),
        out_shape=jax.ShapeDtypeStruct((t, D_MODEL), BF16),
        scratch_shapes=[pltpu.VMEM((tm, GM_WIDTH), BF16), pltpu.VMEM((tm, GM_WIDTH), BF16)],
        compiler_params=_params(("parallel",)),
        name="gmlp",
    )(z, z, z, ln_g, ln_b, w_s, b_s_bcast, w_a)


def _mla_proj_kernel(zl_ref, pos_ref, inv_ref, sgn_ref, wq_ref, wqs_ref, wk_ref, wv_ref, q_ref, k_ref, v_ref):
    scale = (QK_NOPE + QK_ROPE) ** -0.5
    cos_t, sin_t = _rope_tables(pos_ref, inv_ref, sgn_ref)
    cos_t = cos_t * scale
    sin_t = sin_t * scale
    ql = zl_ref[:, 0:Q_LORA]
    kvl = zl_ref[:, Q_LORA:Q_LORA + KV_LORA]
    kpe = zl_ref[:, Q_LORA + KV_LORA:Q_LORA + KV_LORA + LANES]
    qm = jnp.dot(ql, wq_ref[...], preferred_element_type=F32)
    qs = jnp.dot(ql, wqs_ref[...], preferred_element_type=F32)
    kn = jnp.dot(kvl, wk_ref[...], preferred_element_type=F32)
    v_ref[...] = jnp.dot(kvl, wv_ref[...], preferred_element_type=F32).astype(BF16)
    for h in range(MLA_HEADS):
        o = h * HEAD_PAD
        q_ref[:, o:o + LANES] = (qm[:, o:o + LANES] * scale).astype(BF16)
        q_ref[:, o + LANES:o + HEAD_PAD] = (
            qm[:, o + LANES:o + HEAD_PAD] * cos_t + qs[:, h * LANES:(h + 1) * LANES] * sin_t).astype(BF16)
        k_ref[:, o:o + LANES] = kn[:, h * LANES:(h + 1) * LANES].astype(BF16)
        k_ref[:, o + LANES:o + HEAD_PAD] = kpe


def _mla_proj(z, pos2d, inv2, sgn2, wq, wqs, wk, wv, tm):
    t = z.shape[0]
    hw = MLA_HEADS * HEAD_PAD
    const = lambda i: (0, 0)
    return pl.pallas_call(
        _mla_proj_kernel,
        grid=(t // tm,),
        in_specs=[
            pl.BlockSpec((tm, Z_TILE), lambda i: (i, 8)),
            pl.BlockSpec((tm, 1), lambda i: (i, 0)),
            pl.BlockSpec((1, LANES), const),
            pl.BlockSpec((1, LANES), const),
            pl.BlockSpec(wq.shape, const),
            pl.BlockSpec(wqs.shape, const),
            pl.BlockSpec(wk.shape, const),
            pl.BlockSpec(wv.shape, const),
        ],
        out_specs=[
            pl.BlockSpec((tm, hw), lambda i: (i, 0)),
            pl.BlockSpec((tm, hw), lambda i: (i, 0)),
            pl.BlockSpec((tm, MLA_HEADS * V_HEAD), lambda i: (i, 0)),
        ],
        out_shape=[
            jax.ShapeDtypeStruct((t, hw), BF16),
            jax.ShapeDtypeStruct((t, hw), BF16),
            jax.ShapeDtypeStruct((t, MLA_HEADS * V_HEAD), BF16),
        ],
        compiler_params=_params(("parallel",)),
        name="mla_proj",
    )(z, pos2d, inv2, sgn2, wq, wqs, wk, wv)


def _attn_kernel(q_ref, k_ref, v_ref, o_ref):
    tq = q_ref.shape[0]
    iq = pl.program_id(2)
    q = q_ref[...]

    def tile(j, m, l, acc, diagonal):
        start = pl.multiple_of(j * tq, tq)
        k = k_ref[pl.ds(start, tq), :]
        v = v_ref[pl.ds(start, tq), :]
        s = lax.dot_general(q, k, (((1,), (1,)), ((), ())), preferred_element_type=F32)
        if diagonal:
            row = lax.broadcasted_iota(jnp.int32, s.shape, 0)
            col = lax.broadcasted_iota(jnp.int32, s.shape, 1)
            s = jnp.where(col <= row, s, -1e30)
        m_new = jnp.maximum(m, jnp.max(s, axis=-1, keepdims=True))
        p = jnp.exp(s - m_new)
        alpha = jnp.exp(m - m_new)
        l = alpha * l + jnp.sum(p, axis=-1, keepdims=True)
        acc = alpha * acc + jnp.dot(p.astype(BF16), v, preferred_element_type=F32)
        return m_new, l, acc

    init = (jnp.full((tq, 1), -1e30, F32), jnp.zeros((tq, 1), F32), jnp.zeros((tq, V_HEAD), F32))
    m, l, acc = lax.fori_loop(0, iq, lambda j, c: tile(j, *c, False), init)
    m, l, acc = tile(iq, m, l, acc, True)
    o_ref[...] = (acc / l).astype(BF16)


def _attention(q, k, v, bsz, tq):
    t = q.shape[0]
    nq = SEQ // tq
    return pl.pallas_call(
        _attn_kernel,
        grid=(bsz, MLA_HEADS, nq),
        in_specs=[
            pl.BlockSpec((tq, HEAD_PAD), lambda b, h, i: (b * nq + i, h)),
            pl.BlockSpec((SEQ, HEAD_PAD), lambda b, h, i: (b, h)),
            pl.BlockSpec((SEQ, V_HEAD), lambda b, h, i: (b, h)),
        ],
        out_specs=pl.BlockSpec((tq, V_HEAD), lambda b, h, i: (b * nq + i, h)),
        out_shape=jax.ShapeDtypeStruct((t, MLA_HEADS * V_HEAD), BF16),
        compiler_params=_params(("parallel", "parallel", "arbitrary")),
        name="attn",
    )(q, k, v)


def _merge_kernel(o_ref, yag_ref, gb_ref, x_ref, mod_ref, pg_ref, wb_ref, wo_ref, out_ref):
    yb = jnp.dot(o_ref[...], wb_ref[...], preferred_element_type=F32)
    merged = yag_ref[...].astype(F32) + gb_ref[...].astype(F32) * yb
    y = jnp.dot(merged.astype(BF16), wo_ref[...], preferred_element_type=F32)
    out_ref[...] = x_ref[...] + mod_ref[0, 2:3, :] * _rms(y, pg_ref[...])


def _merge(o, yag, z, x2d, mod, post_g, wb, wo, tm):
    t = x2d.shape[0]
    tiles_per_seq = SEQ // tm
    const = lambda i: (0, 0)
    return pl.pallas_call(
        _merge_kernel,
        grid=(t // tm,),
        in_specs=[
            pl.BlockSpec((tm, D_MODEL), lambda i: (i, 0)),
            pl.BlockSpec((tm, D_MODEL), lambda i: (i, 0)),
            pl.BlockSpec((tm, D_MODEL), lambda i: (i, 3)),
            pl.BlockSpec((tm, D_MODEL), lambda i: (i, 0)),
            pl.BlockSpec((1, N_MOD, D_MODEL), lambda i: (i // tiles_per_seq, 0, 0)),
            pl.BlockSpec((1, D_MODEL), const),
            pl.BlockSpec(wb.shape, const),
            pl.BlockSpec(wo.shape, const),
        ],
        out_specs=pl.BlockSpec((tm, D_MODEL), lambda i: (i, 0)),
        out_shape=jax.ShapeDtypeStruct((t, D_MODEL), F32),
        compiler_params=_params(("parallel",)),
        name="merge",
    )(o, yag, z, x2d, mod, post_g, wb, wo)


def _ffn_kernel(x_ref, halo_ref, mod_ref, g_ref, wg_ref, wv_ref, cwg_ref, cwv_ref, cbg_ref, cbv_ref, wd_ref,
                pg_ref, out_ref, h_ref, acc_ref, *, tiles_per_seq):
    tm = x_ref.shape[0]
    i = pl.program_id(0)
    j = pl.program_id(1)

    def modulated(x):
        return _rms(x, g_ref[...]) * (1.0 + mod_ref[0, 4:5, :]) + mod_ref[0, 3:4, :]

    @pl.when(j == 0)
    def _():
        h_ref[HALO:, :] = modulated(x_ref[...]).astype(BF16)
        keep = (i % tiles_per_seq != 0).astype(F32)
        h_ref[0:HALO, :] = (modulated(halo_ref[...]) * keep).astype(BF16)
        acc_ref[...] = jnp.zeros_like(acc_ref)

    hext = h_ref[...]

    def conv(w_ref, cw_ref, cb_ref):
        up = jnp.dot(hext, w_ref[...], preferred_element_type=F32)
        y = cw_ref[0:1, :] * up[HALO - 2:HALO - 2 + tm]
        y = y + cw_ref[1:2, :] * up[HALO - 1:HALO - 1 + tm]
        y = y + cw_ref[2:3, :] * up[HALO:HALO + tm]
        return y + cb_ref[...]

    gate = conv(wg_ref, cwg_ref, cbg_ref)
    val = conv(wv_ref, cwv_ref, cbv_ref)
    act = (gate * jax.nn.sigmoid(gate) * val).astype(BF16)
    acc_ref[...] += jnp.dot(act, wd_ref[...], preferred_element_type=F32)

    @pl.when(j == pl.num_programs(1) - 1)
    def _():
        out_ref[...] = x_ref[...] + mod_ref[0, 5:6, :] * _rms(acc_ref[...], pg_ref[...])


def _ffn(x1, mod, pre_g, w_up, conv_w, conv_b, w_down, post_g, tm, tn):
    t = x1.shape[0]
    tiles_per_seq = SEQ // tm
    nj = D_FF // tn
    halo_blocks = tm // HALO
    return pl.pallas_call(
        functools.partial(_ffn_kernel, tiles_per_seq=tiles_per_seq),
        grid=(t // tm, nj),
        in_specs=[
            pl.BlockSpec((tm, D_MODEL), lambda i, j: (i, 0)),
            pl.BlockSpec((HALO, D_MODEL), lambda i, j: (jnp.maximum(i * halo_blocks - 1, 0), 0)),
            pl.BlockSpec((1, N_MOD, D_MODEL), lambda i, j: (i // tiles_per_seq, 0, 0)),
            pl.BlockSpec((1, D_MODEL), lambda i, j: (0, 0)),
            pl.BlockSpec((D_MODEL, tn), lambda i, j: (0, j)),
            pl.BlockSpec((D_MODEL, tn), lambda i, j: (0, nj + j)),
            pl.BlockSpec((CONV_W, tn), lambda i, j: (0, j)),
            pl.BlockSpec((CONV_W, tn), lambda i, j: (0, nj + j)),
            pl.BlockSpec((1, tn), lambda i, j: (0, j)),
            pl.BlockSpec((1, tn), lambda i, j: (0, nj + j)),
            pl.BlockSpec((tn, D_MODEL), lambda i, j: (j, 0)),
            pl.BlockSpec((1, D_MODEL), lambda i, j: (0, 0)),
        ],
        out_specs=pl.BlockSpec((tm, D_MODEL), lambda i, j: (i, 0)),
        out_shape=jax.ShapeDtypeStruct((t, D_MODEL), F32),
        scratch_shapes=[pltpu.VMEM((HALO + tm, D_MODEL), BF16), pltpu.VMEM((tm, D_MODEL), F32)],
        compiler_params=_params(("parallel", "arbitrary")),
        name="ffn",
    )(x1, x1, mod, pre_g, w_up, w_up, conv_w, conv_w, conv_b, conv_b, w_down, post_g)


def _layout_w_in(w_in):
    u0, v0, q0, kv0, pe0, ga0, gb0 = 0, 2048, 4096, 4608, 4864, 4928, 6976
    half = QK_ROPE // 2
    x1 = w_in[:, pe0:pe0 + half]
    x2 = w_in[:, pe0 + half:pe0 + QK_ROPE]
    zeros = jnp.zeros((D_MODEL, LANES - QK_ROPE), w_in.dtype)
    return jnp.concatenate([
        w_in[:, u0:v0], w_in[:, v0:q0], w_in[:, ga0:gb0], w_in[:, gb0:gb0 + D_MODEL],
        w_in[:, q0:kv0], w_in[:, kv0:pe0], x1, x2, zeros, x2, x1, zeros], axis=1).astype(BF16)


def _layout_w_uq(w_uq):
    w = w_uq.reshape(Q_LORA, MLA_HEADS, QK_NOPE + QK_ROPE)
    half = QK_ROPE // 2
    zeros = jnp.zeros((Q_LORA, MLA_HEADS, LANES - QK_ROPE), w.dtype)
    main = jnp.concatenate([w, zeros], axis=-1).reshape(Q_LORA, MLA_HEADS * HEAD_PAD)
    swap = jnp.concatenate([w[:, :, QK_NOPE + half:], w[:, :, QK_NOPE:QK_NOPE + half], zeros], axis=-1)
    return main.astype(BF16), swap.reshape(Q_LORA, MLA_HEADS * LANES).astype(BF16)


def _layout_w_ukv(w_ukv):
    w = w_ukv.reshape(KV_LORA, MLA_HEADS, QK_NOPE + V_HEAD)
    wk = w[:, :, :QK_NOPE].reshape(KV_LORA, MLA_HEADS * QK_NOPE)
    wv = w[:, :, QK_NOPE:].reshape(KV_LORA, MLA_HEADS * V_HEAD)
    return wk.astype(BF16), wv.astype(BF16)


def kernel(x, c, positions, w_ada, b_ada, pre_norm1_g, w_in, gm_ln_g, gm_ln_b, gm_w_s, gm_b_s, w_branch_a, q_norm_g, w_uq, kv_norm_g, w_ukv, w_branch_b, w_out, post_norm1_g, pre_norm2_g, w_up, conv_w, conv_b, w_down, post_norm2_g):
    bsz, seq, d = x.shape
    t = bsz * seq
    x2d = x.reshape(t, d)
    pos2d = positions.reshape(t, 1)
    row = lambda a: a.reshape(1, -1)

    inv = ROPE_THETA ** (-jnp.arange(0, QK_ROPE, 2, dtype=F32) / QK_ROPE)
    pad = jnp.zeros((LANES - QK_ROPE,), F32)
    inv2 = row(jnp.concatenate([inv, inv, pad]))
    sgn2 = row(jnp.concatenate([-jnp.ones_like(inv), jnp.ones_like(inv), pad]))

    w_z = _layout_w_in(w_in)
    wq, wqs = _layout_w_uq(w_uq)
    wk, wv = _layout_w_ukv(w_ukv)
    b_s_bcast = jnp.broadcast_to(gm_b_s[:, :, None], (GM_GROUPS, CHUNK, CHUNK))

    mod = _ada(c, w_ada, b_ada).reshape(bsz, N_MOD, d)
    z = _inproj(x2d, mod, row(pre_norm1_g), w_z, pos2d, inv2, sgn2, row(q_norm_g), row(kv_norm_g), tm=1024)
    yag = _gmlp(z, row(gm_ln_g), row(gm_ln_b), gm_w_s, b_s_bcast, w_branch_a.astype(BF16), tm=512)
    q, k, v = _mla_proj(z, pos2d, inv2, sgn2, wq, wqs, wk, wv, tm=512)
    o = _attention(q, k, v, bsz, tq=512)
    x1 = _merge(o, yag, z, x2d, mod, row(post_norm1_g), w_branch_b.astype(BF16), w_out.astype(BF16), tm=256)
    x2 = _ffn(x1, mod, row(pre_norm2_g), w_up.astype(BF16), conv_w, row(conv_b), w_down.astype(BF16),
              row(post_norm2_g), tm=512, tn=512)
    return x2.reshape(bsz, seq, d)
```

```python
import functools

import jax
import jax.numpy as jnp
from jax import lax
from jax.experimental import pallas as pl
from jax.experimental.pallas import tpu as pltpu

F32 = jnp.float32
BF16 = jnp.bfloat16

D_MODEL = 2048
SEQ = 2048
GM_WIDTH = 2048
CHUNK = 128
GM_GROUPS = 16
MLA_HEADS = 16
Q_LORA = 512
KV_LORA = 256
QK_NOPE = 128
QK_ROPE = 64
V_HEAD = 128
ROPE_THETA = 10000.0
D_FF = 5632
CONV_W = 3
EPS = 1e-6
N_MOD = 6
LOG2_E = 1.4426950408889634

LANES = 128
HEAD_PAD = 2 * LANES
HALO = 16
VMEM_LIMIT = 56 * 1024 * 1024

Z_TILE = 1024
Z_COLS = 9 * Z_TILE


def _rms(x, g):
    return x * lax.rsqrt(jnp.mean(x * x, axis=-1, keepdims=True) + EPS) * g


NORM_ROWS = 128


def _modulated_rms_to(h_ref, row0, x_ref, gain, shift):
    for r in range(0, x_ref.shape[0], NORM_ROWS):
        n = min(NORM_ROWS, x_ref.shape[0] - r)
        x = x_ref[r:r + n, :]
        inv = lax.rsqrt(jnp.mean(x * x, axis=-1, keepdims=True) + EPS)
        h_ref[row0 + r:row0 + r + n, :] = (x * inv * gain + shift).astype(BF16)


def _residual_rms_to(out_ref, x_ref, y, gain):
    for r in range(0, x_ref.shape[0], NORM_ROWS):
        n = min(NORM_ROWS, x_ref.shape[0] - r)
        yc = y[r:r + n, :]
        inv = lax.rsqrt(jnp.mean(yc * yc, axis=-1, keepdims=True) + EPS)
        out_ref[r:r + n, :] = x_ref[r:r + n, :] + yc * inv * gain


def _rope_tables(pos_ref, inv_ref, sgn_ref):
    ang = pos_ref[...].astype(F32) * inv_ref[...]
    return jnp.cos(ang), jnp.sin(ang) * sgn_ref[...]


def _params(sem):
    return pltpu.CompilerParams(dimension_semantics=sem, vmem_limit_bytes=VMEM_LIMIT)


def _ada_kernel(c_ref, w_ref, b_ref, o_ref):
    c = c_ref[...]
    sc = (c * jax.nn.sigmoid(c)).astype(BF16)
    o_ref[...] = jnp.dot(sc, w_ref[...].astype(BF16), preferred_element_type=F32) + b_ref[...]


def _ada(c, w_ada, b_ada):
    bsz = c.shape[0]
    n = w_ada.shape[1]
    tn = 1024
    return pl.pallas_call(
        _ada_kernel,
        grid=(n // tn,),
        in_specs=[
            pl.BlockSpec((bsz, D_MODEL), lambda j: (0, 0)),
            pl.BlockSpec((D_MODEL, tn), lambda j: (0, j)),
            pl.BlockSpec((1, tn), lambda j: (0, j)),
        ],
        out_specs=pl.BlockSpec((bsz, tn), lambda j: (0, j)),
        out_shape=jax.ShapeDtypeStruct((bsz, n), F32),
        compiler_params=_params(("arbitrary",)),
        name="ada",
    )(c, w_ada, b_ada.reshape(1, n))


N_Z_TILES = Z_COLS // Z_TILE


def _inproj_kernel(x_ref, mod_ref, g_ref, w_ref, z_ref, h_ref):
    j = pl.program_id(1)

    @pl.when(j == 0)
    def _():
        _modulated_rms_to(h_ref, 0, x_ref, g_ref[...] * (1.0 + mod_ref[0, 1:2, :]), mod_ref[0, 0:1, :])

    acc = jnp.dot(h_ref[...], w_ref[...], preferred_element_type=F32)

    @pl.when(j < 4)
    def _():
        c = 0.7978845608028654
        half = 0.5 * acc
        z_ref[...] = (half + half * jnp.tanh(acc * (c + (c * 0.044715) * (acc * acc)))).astype(BF16)

    @pl.when(jnp.logical_and(j >= 4, j < 8))
    def _():
        z_ref[...] = (0.5 + 0.5 * jnp.tanh(0.5 * acc)).astype(BF16)

    @pl.when(j == 8)
    def _():
        z_ref[...] = acc.astype(BF16)


def _inproj(x2d, mod, pre_g, w_z, tm):
    t = x2d.shape[0]
    tiles_per_seq = SEQ // tm
    return pl.pallas_call(
        _inproj_kernel,
        grid=(t // tm, N_Z_TILES),
        in_specs=[
            pl.BlockSpec((tm, D_MODEL), lambda i, j: (i, 0)),
            pl.BlockSpec((1, N_MOD, D_MODEL), lambda i, j: (i // tiles_per_seq, 0, 0)),
            pl.BlockSpec((1, D_MODEL), lambda i, j: (0, 0)),
            pl.BlockSpec((D_MODEL, Z_TILE), lambda i, j: (0, j)),
        ],
        out_specs=pl.BlockSpec((tm, Z_TILE), lambda i, j: (i, j)),
        out_shape=jax.ShapeDtypeStruct((t, Z_COLS), BF16),
        scratch_shapes=[pltpu.VMEM((tm, D_MODEL), BF16)],
        compiler_params=_params(("parallel", "arbitrary")),
        name="inproj",
    )(x2d, mod, pre_g, w_z)


def _gmlp_kernel(u_ref, v_ref, ga_ref, lng_ref, lnb_ref, ws_ref, bs_ref, wa_ref, o_ref, vln_ref, a_ref):
    tm = u_ref.shape[0]
    v = v_ref[...].astype(F32)
    mu = jnp.mean(v, axis=-1, keepdims=True)
    var = jnp.mean(jnp.square(v - mu), axis=-1, keepdims=True)
    vln_ref[...] = ((v - mu) * lax.rsqrt(var + EPS) * lng_ref[...] + lnb_ref[...]).astype(BF16)
    row = lax.broadcasted_iota(jnp.int32, (CHUNK, CHUNK), 0)
    col = lax.broadcasted_iota(jnp.int32, (CHUNK, CHUNK), 1)
    mask = (col <= row).astype(F32)
    for g in range(GM_GROUPS):
        wsm = (ws_ref[g] * mask).astype(BF16)
        cs = slice(g * CHUNK, (g + 1) * CHUNK)
        for c in range(tm // CHUNK):
            rs = slice(c * CHUNK, (c + 1) * CHUNK)
            mixed = jnp.dot(wsm, vln_ref[rs, cs], preferred_element_type=F32) + bs_ref[g]
            a_ref[rs, cs] = (u_ref[rs, cs].astype(F32) * mixed).astype(BF16)
    ya = jnp.dot(a_ref[...], wa_ref[...], preferred_element_type=F32)
    o_ref[...] = (ga_ref[...].astype(F32) * ya).astype(BF16)


def _gmlp(z, ln_g, ln_b, w_s, b_s_bcast, w_a, tm):
    t = z.shape[0]
    return pl.pallas_call(
        _gmlp_kernel,
        grid=(t // tm,),
        in_specs=[
            pl.BlockSpec((tm, GM_WIDTH), lambda i: (i, 0)),
            pl.BlockSpec((tm, GM_WIDTH), lambda i: (i, 1)),
            pl.BlockSpec((tm, D_MODEL), lambda i: (i, 2)),
            pl.BlockSpec((1, GM_WIDTH), lambda i: (0, 0)),
            pl.BlockSpec((1, GM_WIDTH), lambda i: (0, 0)),
            pl.BlockSpec((GM_GROUPS, CHUNK, CHUNK), lambda i: (0, 0, 0)),
            pl.BlockSpec((GM_GROUPS, CHUNK, CHUNK), lambda i: (0, 0, 0)),
            pl.BlockSpec((GM_WIDTH, D_MODEL), lambda i: (0, 0)),
        ],
        out_specs=pl.BlockSpec((tm, D_MODEL), lambda i: (i, 0)),
        out_shape=jax.ShapeDtypeStruct((t, D_MODEL), BF16),
        scratch_shapes=[pltpu.VMEM((tm, GM_WIDTH), BF16), pltpu.VMEM((tm, GM_WIDTH), BF16)],
        compiler_params=_params(("parallel",)),
        name="gmlp",
    )(z, z, z, ln_g, ln_b, w_s, b_s_bcast, w_a)


def _mla_proj_kernel(zl_ref, pos_ref, inv_ref, sgn_ref, qg_ref, kvg_ref, wq_ref, wqs_ref, wk_ref, wv_ref,
                     q_ref, k_ref, v_ref):
    scale = LOG2_E * (QK_NOPE + QK_ROPE) ** -0.5
    cos_t, sin_t = _rope_tables(pos_ref, inv_ref, sgn_ref)
    o = Q_LORA + KV_LORA
    ql = _rms(zl_ref[:, 0:Q_LORA].astype(F32), qg_ref[...]).astype(BF16)
    kvl = _rms(zl_ref[:, Q_LORA:o].astype(F32), kvg_ref[...]).astype(BF16)
    kpe = (zl_ref[:, o:o + LANES].astype(F32) * cos_t + zl_ref[:, o + LANES:o + 2 * LANES].astype(F32) * sin_t)
    kpe = kpe.astype(BF16)
    cos_t = cos_t * scale
    sin_t = sin_t * scale
    qm = jnp.dot(ql, wq_ref[...], preferred_element_type=F32)
    qs = jnp.dot(ql, wqs_ref[...], preferred_element_type=F32)
    kn = jnp.dot(kvl, wk_ref[...], preferred_element_type=F32)
    v_ref[...] = jnp.dot(kvl, wv_ref[...], preferred_element_type=F32).astype(BF16)
    for h in range(MLA_HEADS):
        o = h * HEAD_PAD
        q_ref[:, o:o + LANES] = (qm[:, o:o + LANES] * scale).astype(BF16)
        q_ref[:, o + LANES:o + HEAD_PAD] = (
            qm[:, o + LANES:o + HEAD_PAD] * cos_t + qs[:, h * LANES:(h + 1) * LANES] * sin_t).astype(BF16)
        k_ref[:, o:o + LANES] = kn[:, h * LANES:(h + 1) * LANES].astype(BF16)
        k_ref[:, o + LANES:o + HEAD_PAD] = kpe


def _mla_proj(z, pos2d, inv2, sgn2, q_g, kv_g, wq, wqs, wk, wv, tm):
    t = z.shape[0]
    hw = MLA_HEADS * HEAD_PAD
    const = lambda i: (0, 0)
    return pl.pallas_call(
        _mla_proj_kernel,
        grid=(t // tm,),
        in_specs=[
            pl.BlockSpec((tm, Z_TILE), lambda i: (i, 8)),
            pl.BlockSpec((tm, 1), lambda i: (i, 0)),
            pl.BlockSpec((1, LANES), const),
            pl.BlockSpec((1, LANES), const),
            pl.BlockSpec((1, Q_LORA), const),
            pl.BlockSpec((1, KV_LORA), const),
            pl.BlockSpec(wq.shape, const),
            pl.BlockSpec(wqs.shape, const),
            pl.BlockSpec(wk.shape, const),
            pl.BlockSpec(wv.shape, const),
        ],
        out_specs=[
            pl.BlockSpec((tm, hw), lambda i: (i, 0)),
            pl.BlockSpec((tm, hw), lambda i: (i, 0)),
            pl.BlockSpec((tm, MLA_HEADS * V_HEAD), lambda i: (i, 0)),
        ],
        out_shape=[
            jax.ShapeDtypeStruct((t, hw), BF16),
            jax.ShapeDtypeStruct((t, hw), BF16),
            jax.ShapeDtypeStruct((t, MLA_HEADS * V_HEAD), BF16),
        ],
        compiler_params=_params(("parallel",)),
        name="mla_proj",
    )(z, pos2d, inv2, sgn2, q_g, kv_g, wq, wqs, wk, wv)


def _attn_kernel(q_ref, k_ref, v_ref, o_ref, *, tq):
    nt = (((1,), (1,)), ((), ()))
    row = lax.broadcasted_iota(jnp.int32, (tq, tq), 0)
    col = lax.broadcasted_iota(jnp.int32, (tq, tq), 1)
    for i in range(SEQ // tq):
        lo = i * tq
        q = q_ref[lo:lo + tq, :]
        s_d = lax.dot_general(q, k_ref[lo:lo + tq, :], nt, preferred_element_type=F32)
        s_d = jnp.where(col <= row, s_d, -1e30)
        m = jnp.max(s_d, axis=-1, keepdims=True)
        if i > 0:
            s_o = lax.dot_general(q, k_ref[0:lo, :], nt, preferred_element_type=F32)
            m = jnp.maximum(m, jnp.max(s_o, axis=-1, keepdims=True))
        p_d = jnp.exp2(s_d - m)
        l = jnp.sum(p_d, axis=-1, keepdims=True)
        acc = jnp.dot(p_d.astype(BF16), v_ref[lo:lo + tq, :], preferred_element_type=F32)
        if i > 0:
            p_o = jnp.exp2(s_o - m)
            l = l + jnp.sum(p_o, axis=-1, keepdims=True)
            acc = acc + jnp.dot(p_o.astype(BF16), v_ref[0:lo, :], preferred_element_type=F32)
        o_ref[lo:lo + tq, :] = (acc / l).astype(BF16)


def _attention(q, k, v, bsz, tq):
    t = q.shape[0]
    return pl.pallas_call(
        functools.partial(_attn_kernel, tq=tq),
        grid=(bsz, MLA_HEADS),
        in_specs=[
            pl.BlockSpec((SEQ, HEAD_PAD), lambda b, h: (b, h)),
            pl.BlockSpec((SEQ, HEAD_PAD), lambda b, h: (b, h)),
            pl.BlockSpec((SEQ, V_HEAD), lambda b, h: (b, h)),
        ],
        out_specs=pl.BlockSpec((SEQ, V_HEAD), lambda b, h: (b, h)),
        out_shape=jax.ShapeDtypeStruct((t, MLA_HEADS * V_HEAD), BF16),
        compiler_params=_params(("parallel", "parallel")),
        name="attn",
    )(q, k, v)


def _merge_kernel(o_ref, yag_ref, gb_ref, x_ref, mod_ref, pg_ref, wb_ref, wo_ref, out_ref):
    yb = jnp.dot(o_ref[...], wb_ref[...], preferred_element_type=F32)
    merged = yag_ref[...].astype(F32) + gb_ref[...].astype(F32) * yb
    y = jnp.dot(merged.astype(BF16), wo_ref[...], preferred_element_type=F32)
    _residual_rms_to(out_ref, x_ref, y, mod_ref[0, 2:3, :] * pg_ref[...])


def _merge(o, yag, z, x2d, mod, post_g, wb, wo, tm):
    t = x2d.shape[0]
    tiles_per_seq = SEQ // tm
    const = lambda i: (0, 0)
    return pl.pallas_call(
        _merge_kernel,
        grid=(t // tm,),
        in_specs=[
            pl.BlockSpec((tm, D_MODEL), lambda i: (i, 0)),
            pl.BlockSpec((tm, D_MODEL), lambda i: (i, 0)),
            pl.BlockSpec((tm, D_MODEL), lambda i: (i, 3)),
            pl.BlockSpec((tm, D_MODEL), lambda i: (i, 0)),
            pl.BlockSpec((1, N_MOD, D_MODEL), lambda i: (i // tiles_per_seq, 0, 0)),
            pl.BlockSpec((1, D_MODEL), const),
            pl.BlockSpec(wb.shape, const),
            pl.BlockSpec(wo.shape, const),
        ],
        out_specs=pl.BlockSpec((tm, D_MODEL), lambda i: (i, 0)),
        out_shape=jax.ShapeDtypeStruct((t, D_MODEL), F32),
        compiler_params=_params(("parallel",)),
        name="merge",
    )(o, yag, z, x2d, mod, post_g, wb, wo)


def _ffn_kernel(x_ref, halo_ref, mod_ref, g_ref, wg_ref, wv_ref, cwg_ref, cwv_ref, cbg_ref, cbv_ref, wd_ref,
                pg_ref, out_ref, h_ref, acc_ref, *, tiles_per_seq):
    tm = x_ref.shape[0]
    i = pl.program_id(0)
    j = pl.program_id(1)

    @pl.when(j == 0)
    def _():
        gain = g_ref[...] * (1.0 + mod_ref[0, 4:5, :])
        shift = mod_ref[0, 3:4, :]
        _modulated_rms_to(h_ref, HALO, x_ref, gain, shift)
        keep = (i % tiles_per_seq != 0).astype(F32)
        _modulated_rms_to(h_ref, 0, halo_ref, gain * keep, shift * keep)
        acc_ref[...] = jnp.zeros_like(acc_ref)

    hext = h_ref[...]

    def conv(w_ref, cw_ref, cb_ref):
        up = jnp.dot(hext, w_ref[...], preferred_element_type=F32)
        y = cw_ref[0:1, :] * up[HALO - 2:HALO - 2 + tm]
        y = y + cw_ref[1:2, :] * up[HALO - 1:HALO - 1 + tm]
        y = y + cw_ref[2:3, :] * up[HALO:HALO + tm]
        return y + cb_ref[...]

    gate = conv(wg_ref, cwg_ref, cbg_ref)
    val = conv(wv_ref, cwv_ref, cbv_ref)
    act = (gate * jax.nn.sigmoid(gate) * val).astype(BF16)
    acc_ref[...] += jnp.dot(act, wd_ref[...], preferred_element_type=F32)

    @pl.when(j == pl.num_programs(1) - 1)
    def _():
        _residual_rms_to(out_ref, x_ref, acc_ref, mod_ref[0, 5:6, :] * pg_ref[...])


def _ffn(x1, mod, pre_g, w_up, conv_w, conv_b, w_down, post_g, tm, tn):
    t = x1.shape[0]
    tiles_per_seq = SEQ // tm
    nj = D_FF // tn
    halo_blocks = tm // HALO
    return pl.pallas_call(
        functools.partial(_ffn_kernel, tiles_per_seq=tiles_per_seq),
        grid=(t // tm, nj),
        in_specs=[
            pl.BlockSpec((tm, D_MODEL), lambda i, j: (i, 0)),
            pl.BlockSpec((HALO, D_MODEL), lambda i, j: (jnp.maximum(i * halo_blocks - 1, 0), 0)),
            pl.BlockSpec((1, N_MOD, D_MODEL), lambda i, j: (i // tiles_per_seq, 0, 0)),
            pl.BlockSpec((1, D_MODEL), lambda i, j: (0, 0)),
            pl.BlockSpec((D_MODEL, tn), lambda i, j: (0, j)),
            pl.BlockSpec((D_MODEL, tn), lambda i, j: (0, nj + j)),
            pl.BlockSpec((CONV_W, tn), lambda i, j: (0, j)),
            pl.BlockSpec((CONV_W, tn), lambda i, j: (0, nj + j)),
            pl.BlockSpec((1, tn), lambda i, j: (0, j)),
            pl.BlockSpec((1, tn), lambda i, j: (0, nj + j)),
            pl.BlockSpec((tn, D_MODEL), lambda i, j: (j, 0)),
            pl.BlockSpec((1, D_MODEL), lambda i, j: (0, 0)),
        ],
        out_specs=pl.BlockSpec((tm, D_MODEL), lambda i, j: (i, 0)),
        out_shape=jax.ShapeDtypeStruct((t, D_MODEL), F32),
        scratch_shapes=[pltpu.VMEM((HALO + tm, D_MODEL), BF16), pltpu.VMEM((tm, D_MODEL), F32)],
        compiler_params=_params(("parallel", "arbitrary")),
        name="ffn",
    )(x1, x1, mod, pre_g, w_up, w_up, conv_w, conv_w, conv_b, conv_b, w_down, post_g)


def _layout_w_in(w_in):
    u0, v0, q0, kv0, pe0, ga0, gb0 = 0, 2048, 4096, 4608, 4864, 4928, 6976
    half = QK_ROPE // 2
    x1 = w_in[:, pe0:pe0 + half]
    x2 = w_in[:, pe0 + half:pe0 + QK_ROPE]
    zeros = jnp.zeros((D_MODEL, LANES - QK_ROPE), w_in.dtype)
    return jnp.concatenate([
        w_in[:, u0:v0], w_in[:, v0:q0], w_in[:, ga0:gb0], w_in[:, gb0:gb0 + D_MODEL],
        w_in[:, q0:kv0], w_in[:, kv0:pe0], x1, x2, zeros, x2, x1, zeros], axis=1).astype(BF16)


def _layout_w_uq(w_uq):
    w = w_uq.reshape(Q_LORA, MLA_HEADS, QK_NOPE + QK_ROPE)
    half = QK_ROPE // 2
    zeros = jnp.zeros((Q_LORA, MLA_HEADS, LANES - QK_ROPE), w.dtype)
    main = jnp.concatenate([w, zeros], axis=-1).reshape(Q_LORA, MLA_HEADS * HEAD_PAD)
    swap = jnp.concatenate([w[:, :, QK_NOPE + half:], w[:, :, QK_NOPE:QK_NOPE + half], zeros], axis=-1)
    return main.astype(BF16), swap.reshape(Q_LORA, MLA_HEADS * LANES).astype(BF16)


def _layout_w_ukv(w_ukv):
    w = w_ukv.reshape(KV_LORA, MLA_HEADS, QK_NOPE + V_HEAD)
    wk = w[:, :, :QK_NOPE].reshape(KV_LORA, MLA_HEADS * QK_NOPE)
    wv = w[:, :, QK_NOPE:].reshape(KV_LORA, MLA_HEADS * V_HEAD)
    return wk.astype(BF16), wv.astype(BF16)


def kernel(x, c, positions, w_ada, b_ada, pre_norm1_g, w_in, gm_ln_g, gm_ln_b, gm_w_s, gm_b_s, w_branch_a, q_norm_g, w_uq, kv_norm_g, w_ukv, w_branch_b, w_out, post_norm1_g, pre_norm2_g, w_up, conv_w, conv_b, w_down, post_norm2_g):
    bsz, seq, d = x.shape
    t = bsz * seq
    x2d = x.reshape(t, d)
    pos2d = positions.reshape(t, 1)
    row = lambda a: a.reshape(1, -1)

    inv = ROPE_THETA ** (-jnp.arange(0, QK_ROPE, 2, dtype=F32) / QK_ROPE)
    pad = jnp.zeros((LANES - QK_ROPE,), F32)
    inv2 = row(jnp.concatenate([inv, inv, pad]))
    sgn2 = row(jnp.concatenate([-jnp.ones_like(inv), jnp.ones_like(inv), pad]))

    w_z = _layout_w_in(w_in)
    wq, wqs = _layout_w_uq(w_uq)
    wk, wv = _layout_w_ukv(w_ukv)
    b_s_bcast = jnp.broadcast_to(gm_b_s[:, :, None], (GM_GROUPS, CHUNK, CHUNK))

    mod = _ada(c, w_ada, b_ada).reshape(bsz, N_MOD, d)
    z = _inproj(x2d, mod, row(pre_norm1_g), w_z, tm=1024)
    yag = _gmlp(z, row(gm_ln_g), row(gm_ln_b), gm_w_s, b_s_bcast, w_branch_a.astype(BF16), tm=512)
    q, k, v = _mla_proj(z, pos2d, inv2, sgn2, row(q_norm_g), row(kv_norm_g), wq, wqs, wk, wv, tm=512)
    o = _attention(q, k, v, bsz, tq=512)
    x1 = _merge(o, yag, z, x2d, mod, row(post_norm1_g), w_branch_b.astype(BF16), w_out.astype(BF16), tm=256)
    x2 = _ffn(x1, mod, row(pre_norm2_g), w_up.astype(BF16), conv_w, row(conv_b), w_down.astype(BF16),
              row(post_norm2_g), tm=512, tn=512)
    return x2.reshape(bsz, seq, d)
```

```python
import functools

import jax
import jax.numpy as jnp
from jax import lax
from jax.experimental import pallas as pl
from jax.experimental.pallas import tpu as pltpu

F32 = jnp.float32
BF16 = jnp.bfloat16

D_MODEL = 2048
SEQ = 2048
GM_WIDTH = 2048
CHUNK = 128
GM_GROUPS = 16
MLA_HEADS = 16
Q_LORA = 512
KV_LORA = 256
QK_NOPE = 128
QK_ROPE = 64
V_HEAD = 128
ROPE_THETA = 10000.0
D_FF = 5632
CONV_W = 3
EPS = 1e-6
N_MOD = 6
LOG2_E = 1.4426950408889634

LANES = 128
HEAD_PAD = 2 * LANES
HALO = 16
VMEM_LIMIT = 56 * 1024 * 1024

Z_TILE = 1024
Z_COLS = 9 * Z_TILE


def _rms(x, g):
    return x * lax.rsqrt(jnp.mean(x * x, axis=-1, keepdims=True) + EPS) * g


NORM_ROWS = 128
GMLP_ROWS = 256
MLA_ROWS = 256

def _modulated_rms_to(h_ref, row0, x_ref, gain, shift):
    for r in range(0, x_ref.shape[0], NORM_ROWS):
        n = min(NORM_ROWS, x_ref.shape[0] - r)
        x = x_ref[r:r + n, :]
        inv = lax.rsqrt(jnp.mean(x * x, axis=-1, keepdims=True) + EPS)
        h_ref[row0 + r:row0 + r + n, :] = (x * inv * gain + shift).astype(BF16)


def _residual_rms_to(out_ref, x_ref, y, gain):
    for r in range(0, x_ref.shape[0], NORM_ROWS):
        n = min(NORM_ROWS, x_ref.shape[0] - r)
        yc = y[r:r + n, :]
        inv = lax.rsqrt(jnp.mean(yc * yc, axis=-1, keepdims=True) + EPS)
        out_ref[r:r + n, :] = x_ref[r:r + n, :] + yc * inv * gain


def _params(sem):
    return pltpu.CompilerParams(dimension_semantics=sem, vmem_limit_bytes=VMEM_LIMIT)


def _ada_kernel(c_ref, w_ref, b_ref, o_ref):
    c = c_ref[...]
    sc = (c * jax.nn.sigmoid(c)).astype(BF16)
    o_ref[...] = jnp.dot(sc, w_ref[...].astype(BF16), preferred_element_type=F32) + b_ref[...]


def _ada(c, w_ada, b_ada):
    bsz = c.shape[0]
    n = w_ada.shape[1]
    tn = 1024
    return pl.pallas_call(
        _ada_kernel,
        grid=(n // tn,),
        in_specs=[
            pl.BlockSpec((bsz, D_MODEL), lambda j: (0, 0)),
            pl.BlockSpec((D_MODEL, tn), lambda j: (0, j)),
            pl.BlockSpec((1, tn), lambda j: (0, j)),
        ],
        out_specs=pl.BlockSpec((bsz, tn), lambda j: (0, j)),
        out_shape=jax.ShapeDtypeStruct((bsz, n), F32),
        compiler_params=_params(("arbitrary",)),
        name="ada",
    )(c, w_ada, b_ada.reshape(1, n))


N_Z_TILES = Z_COLS // Z_TILE


def _inproj_kernel(x_ref, mod_ref, g_ref, w_ref, z_ref, h_ref):
    j = pl.program_id(1)

    @pl.when(j == 0)
    def _():
        _modulated_rms_to(h_ref, 0, x_ref, g_ref[...] * (1.0 + mod_ref[0, 1:2, :]), mod_ref[0, 0:1, :])

    acc = jnp.dot(h_ref[...], w_ref[...], preferred_element_type=F32)

    @pl.when(j < 4)
    def _():
        c = 0.7978845608028654
        half = 0.5 * acc
        z_ref[...] = (half + half * jnp.tanh(acc * (c + (c * 0.044715) * (acc * acc)))).astype(BF16)

    @pl.when(jnp.logical_and(j >= 4, j < 8))
    def _():
        z_ref[...] = (0.5 + 0.5 * jnp.tanh(0.5 * acc)).astype(BF16)

    @pl.when(j == 8)
    def _():
        z_ref[...] = acc.astype(BF16)


def _inproj(x2d, mod, pre_g, w_z, tm):
    t = x2d.shape[0]
    tiles_per_seq = SEQ // tm
    return pl.pallas_call(
        _inproj_kernel,
        grid=(t // tm, N_Z_TILES),
        in_specs=[
            pl.BlockSpec((tm, D_MODEL), lambda i, j: (i, 0)),
            pl.BlockSpec((1, N_MOD, D_MODEL), lambda i, j: (i // tiles_per_seq, 0, 0)),
            pl.BlockSpec((1, D_MODEL), lambda i, j: (0, 0)),
            pl.BlockSpec((D_MODEL, Z_TILE), lambda i, j: (0, j)),
        ],
        out_specs=pl.BlockSpec((tm, Z_TILE), lambda i, j: (i, j)),
        out_shape=jax.ShapeDtypeStruct((t, Z_COLS), BF16),
        scratch_shapes=[pltpu.VMEM((tm, D_MODEL), BF16)],
        compiler_params=_params(("parallel", "arbitrary")),
        name="inproj",
    )(x2d, mod, pre_g, w_z)


def _gmlp_kernel(u_ref, v_ref, ga_ref, lng_ref, lnb_ref, ws_ref, bs_ref, wa_ref, o_ref, vln_ref, a_ref):
    tm = u_ref.shape[0]
    row = lax.broadcasted_iota(jnp.int32, (CHUNK, CHUNK), 0)
    col = lax.broadcasted_iota(jnp.int32, (CHUNK, CHUNK), 1)
    mask = (col <= row).astype(F32)
    wsm = [(ws_ref[g] * mask).astype(BF16) for g in range(GM_GROUPS)]
    for r0 in range(0, tm, GMLP_ROWS):
        for c0 in range(r0, r0 + GMLP_ROWS, CHUNK):
            rs = slice(c0, c0 + CHUNK)
            v = v_ref[rs, :].astype(F32)
            mu = jnp.mean(v, axis=-1, keepdims=True)
            var = jnp.mean(jnp.square(v - mu), axis=-1, keepdims=True)
            vln_ref[rs, :] = ((v - mu) * lax.rsqrt(var + EPS) * lng_ref[...] + lnb_ref[...]).astype(BF16)
            for g in range(GM_GROUPS):
                cs = slice(g * CHUNK, (g + 1) * CHUNK)
                mixed = jnp.dot(wsm[g], vln_ref[rs, cs], preferred_element_type=F32) + bs_ref[g]
                a_ref[rs, cs] = (u_ref[rs, cs].astype(F32) * mixed).astype(BF16)
        rg = slice(r0, r0 + GMLP_ROWS)
        ya = jnp.dot(a_ref[rg, :], wa_ref[...], preferred_element_type=F32)
        o_ref[rg, :] = (ga_ref[rg, :].astype(F32) * ya).astype(BF16)


def _gmlp(z, ln_g, ln_b, w_s, b_s_bcast, w_a, tm):
    t = z.shape[0]
    return pl.pallas_call(
        _gmlp_kernel,
        grid=(t // tm,),
        in_specs=[
            pl.BlockSpec((tm, GM_WIDTH), lambda i: (i, 0)),
            pl.BlockSpec((tm, GM_WIDTH), lambda i: (i, 1)),
            pl.BlockSpec((tm, D_MODEL), lambda i: (i, 2)),
            pl.BlockSpec((1, GM_WIDTH), lambda i: (0, 0)),
            pl.BlockSpec((1, GM_WIDTH), lambda i: (0, 0)),
            pl.BlockSpec((GM_GROUPS, CHUNK, CHUNK), lambda i: (0, 0, 0)),
            pl.BlockSpec((GM_GROUPS, CHUNK, CHUNK), lambda i: (0, 0, 0)),
            pl.BlockSpec((GM_WIDTH, D_MODEL), lambda i: (0, 0)),
        ],
        out_specs=pl.BlockSpec((tm, D_MODEL), lambda i: (i, 0)),
        out_shape=jax.ShapeDtypeStruct((t, D_MODEL), BF16),
        scratch_shapes=[pltpu.VMEM((tm, GM_WIDTH), BF16), pltpu.VMEM((tm, GM_WIDTH), BF16)],
        compiler_params=_params(("parallel",)),
        name="gmlp",
    )(z, z, z, ln_g, ln_b, w_s, b_s_bcast, w_a)


def _mla_proj_kernel(zl_ref, pos_ref, inv_ref, sgn_ref, qg_ref, kvg_ref, wq_ref, wqs_ref, wk_ref, wv_ref,
                     q_ref, k_ref, v_ref):
    scale = LOG2_E * (QK_NOPE + QK_ROPE) ** -0.5
    lat = Q_LORA + KV_LORA
    for r0 in range(0, zl_ref.shape[0], MLA_ROWS):
        rs = slice(r0, r0 + MLA_ROWS)
        ang = pos_ref[rs, :].astype(F32) * inv_ref[...]
        cos_t = jnp.cos(ang)
        sin_t = jnp.sin(ang) * sgn_ref[...]
        ql = _rms(zl_ref[rs, 0:Q_LORA].astype(F32), qg_ref[...]).astype(BF16)
        kvl = _rms(zl_ref[rs, Q_LORA:lat].astype(F32), kvg_ref[...]).astype(BF16)
        kpe = (zl_ref[rs, lat:lat + LANES].astype(F32) * cos_t
               + zl_ref[rs, lat + LANES:lat + 2 * LANES].astype(F32) * sin_t).astype(BF16)
        cos_t = cos_t * scale
        sin_t = sin_t * scale
        qm = jnp.dot(ql, wq_ref[...], preferred_element_type=F32)
        qs = jnp.dot(ql, wqs_ref[...], preferred_element_type=F32)
        kn = jnp.dot(kvl, wk_ref[...], preferred_element_type=F32)
        v_ref[rs, :] = jnp.dot(kvl, wv_ref[...], preferred_element_type=F32).astype(BF16)
        for h in range(MLA_HEADS):
            o = h * HEAD_PAD
            q_ref[rs, o:o + LANES] = (qm[:, o:o + LANES] * scale).astype(BF16)
            q_ref[rs, o + LANES:o + HEAD_PAD] = (
                qm[:, o + LANES:o + HEAD_PAD] * cos_t + qs[:, h * LANES:(h + 1) * LANES] * sin_t).astype(BF16)
            k_ref[rs, o:o + LANES] = kn[:, h * LANES:(h + 1) * LANES].astype(BF16)
            k_ref[rs, o + LANES:o + HEAD_PAD] = kpe


def _mla_proj(z, pos2d, inv2, sgn2, q_g, kv_g, wq, wqs, wk, wv, tm):
    t = z.shape[0]
    hw = MLA_HEADS * HEAD_PAD
    const = lambda i: (0, 0)
    return pl.pallas_call(
        _mla_proj_kernel,
        grid=(t // tm,),
        in_specs=[
            pl.BlockSpec((tm, Z_TILE), lambda i: (i, 8)),
            pl.BlockSpec((tm, 1), lambda i: (i, 0)),
            pl.BlockSpec((1, LANES), const),
            pl.BlockSpec((1, LANES), const),
            pl.BlockSpec((1, Q_LORA), const),
            pl.BlockSpec((1, KV_LORA), const),
            pl.BlockSpec(wq.shape, const),
            pl.BlockSpec(wqs.shape, const),
            pl.BlockSpec(wk.shape, const),
            pl.BlockSpec(wv.shape, const),
        ],
        out_specs=[
            pl.BlockSpec((tm, hw), lambda i: (i, 0)),
            pl.BlockSpec((tm, hw), lambda i: (i, 0)),
            pl.BlockSpec((tm, MLA_HEADS * V_HEAD), lambda i: (i, 0)),
        ],
        out_shape=[
            jax.ShapeDtypeStruct((t, hw), BF16),
            jax.ShapeDtypeStruct((t, hw), BF16),
            jax.ShapeDtypeStruct((t, MLA_HEADS * V_HEAD), BF16),
        ],
        compiler_params=_params(("parallel",)),
        name="mla_proj",
    )(z, pos2d, inv2, sgn2, q_g, kv_g, wq, wqs, wk, wv)


def _attn_kernel(q_ref, k_ref, v_ref, o_ref, *, tq):
    nt = (((1,), (1,)), ((), ()))
    row = lax.broadcasted_iota(jnp.int32, (tq, tq), 0)
    col = lax.broadcasted_iota(jnp.int32, (tq, tq), 1)
    for i in reversed(range(SEQ // tq)):
        lo = i * tq
        q = q_ref[lo:lo + tq, :]
        s_d = lax.dot_general(q, k_ref[lo:lo + tq, :], nt, preferred_element_type=F32)
        s_d = jnp.where(col <= row, s_d, -1e30)
        m = jnp.max(s_d, axis=-1, keepdims=True)
        if i > 0:
            s_o = lax.dot_general(q, k_ref[0:lo, :], nt, preferred_element_type=F32)
            m = jnp.maximum(m, jnp.max(s_o, axis=-1, keepdims=True))
        p_d = jnp.exp2(s_d - m)
        l = jnp.sum(p_d, axis=-1, keepdims=True)
        acc = jnp.dot(p_d.astype(BF16), v_ref[lo:lo + tq, :], preferred_element_type=F32)
        if i > 0:
            p_o = jnp.exp2(s_o - m)
            l = l + jnp.sum(p_o, axis=-1, keepdims=True)
            acc = acc + jnp.dot(p_o.astype(BF16), v_ref[0:lo, :], preferred_element_type=F32)
        o_ref[lo:lo + tq, :] = (acc / l).astype(BF16)


def _attention(q, k, v, bsz, tq):
    t = q.shape[0]
    return pl.pallas_call(
        functools.partial(_attn_kernel, tq=tq),
        grid=(bsz, MLA_HEADS),
        in_specs=[
            pl.BlockSpec((SEQ, HEAD_PAD), lambda b, h: (b, h)),
            pl.BlockSpec((SEQ, HEAD_PAD), lambda b, h: (b, h)),
            pl.BlockSpec((SEQ, V_HEAD), lambda b, h: (b, h)),
        ],
        out_specs=pl.BlockSpec((SEQ, V_HEAD), lambda b, h: (b, h)),
        out_shape=jax.ShapeDtypeStruct((t, MLA_HEADS * V_HEAD), BF16),
        compiler_params=_params(("parallel", "parallel")),
        name="attn",
    )(q, k, v)


def _merge_kernel(o_ref, yag_ref, gb_ref, x_ref, mod_ref, pg_ref, wb_ref, wo_ref, out_ref):
    yb = jnp.dot(o_ref[...], wb_ref[...], preferred_element_type=F32)
    merged = yag_ref[...].astype(F32) + gb_ref[...].astype(F32) * yb
    y = jnp.dot(merged.astype(BF16), wo_ref[...], preferred_element_type=F32)
    _residual_rms_to(out_ref, x_ref, y, mod_ref[0, 2:3, :] * pg_ref[...])


def _merge(o, yag, z, x2d, mod, post_g, wb, wo, tm):
    t = x2d.shape[0]
    tiles_per_seq = SEQ // tm
    const = lambda i: (0, 0)
    return pl.pallas_call(
        _merge_kernel,
        grid=(t // tm,),
        in_specs=[
            pl.BlockSpec((tm, D_MODEL), lambda i: (i, 0)),
            pl.BlockSpec((tm, D_MODEL), lambda i: (i, 0)),
            pl.BlockSpec((tm, D_MODEL), lambda i: (i, 3)),
            pl.BlockSpec((tm, D_MODEL), lambda i: (i, 0)),
            pl.BlockSpec((1, N_MOD, D_MODEL), lambda i: (i // tiles_per_seq, 0, 0)),
            pl.BlockSpec((1, D_MODEL), const),
            pl.BlockSpec(wb.shape, const),
            pl.BlockSpec(wo.shape, const),
        ],
        out_specs=pl.BlockSpec((tm, D_MODEL), lambda i: (i, 0)),
        out_shape=jax.ShapeDtypeStruct((t, D_MODEL), F32),
        compiler_params=_params(("parallel",)),
        name="merge",
    )(o, yag, z, x2d, mod, post_g, wb, wo)


def _ffn_up_kernel(x_ref, halo_ref, mod_ref, g_ref, wg_ref, wv_ref, cwg_ref, cwv_ref, cbg_ref, cbv_ref,
                   act_ref, h_ref, *, tiles_per_seq):
    tm = x_ref.shape[0]
    i = pl.program_id(0)

    @pl.when(pl.program_id(1) == 0)
    def _():
        gain = g_ref[...] * (1.0 + mod_ref[0, 4:5, :])
        shift = mod_ref[0, 3:4, :]
        _modulated_rms_to(h_ref, HALO, x_ref, gain, shift)
        keep = (i % tiles_per_seq != 0).astype(F32)
        _modulated_rms_to(h_ref, 0, halo_ref, gain * keep, shift * keep)

    hext = h_ref[...]

    def conv(w_ref, cw_ref, cb_ref):
        up = jnp.dot(hext, w_ref[...].astype(BF16), preferred_element_type=F32)
        y = cw_ref[0:1, :] * pltpu.roll(up, 2, axis=0)[HALO:HALO + tm, :]
        y = y + cw_ref[1:2, :] * pltpu.roll(up, 1, axis=0)[HALO:HALO + tm, :]
        y = y + cw_ref[2:3, :] * up[HALO:HALO + tm, :]
        return y + cb_ref[...]

    gate = conv(wg_ref, cwg_ref, cbg_ref)
    val = conv(wv_ref, cwv_ref, cbv_ref)
    act_ref[...] = (gate * jax.nn.sigmoid(gate) * val).astype(BF16)


def _ffn_up(x1, mod, pre_g, w_up, conv_w, conv_b, tm, tn):
    t = x1.shape[0]
    tiles_per_seq = SEQ // tm
    nj = D_FF // tn
    halo_blocks = tm // HALO
    return pl.pallas_call(
        functools.partial(_ffn_up_kernel, tiles_per_seq=tiles_per_seq),
        grid=(t // tm, nj),
        in_specs=[
            pl.BlockSpec((tm, D_MODEL), lambda i, j: (i, 0)),
            pl.BlockSpec((HALO, D_MODEL), lambda i, j: (jnp.maximum(i * halo_blocks - 1, 0), 0)),
            pl.BlockSpec((1, N_MOD, D_MODEL), lambda i, j: (i // tiles_per_seq, 0, 0)),
            pl.BlockSpec((1, D_MODEL), lambda i, j: (0, 0)),
            pl.BlockSpec((D_MODEL, tn), lambda i, j: (0, j)),
            pl.BlockSpec((D_MODEL, tn), lambda i, j: (0, nj + j)),
            pl.BlockSpec((CONV_W, tn), lambda i, j: (0, j)),
            pl.BlockSpec((CONV_W, tn), lambda i, j: (0, nj + j)),
            pl.BlockSpec((1, tn), lambda i, j: (0, j)),
            pl.BlockSpec((1, tn), lambda i, j: (0, nj + j)),
        ],
        out_specs=pl.BlockSpec((tm, tn), lambda i, j: (i, j)),
        out_shape=jax.ShapeDtypeStruct((t, D_FF), BF16),
        scratch_shapes=[pltpu.VMEM((HALO + tm, D_MODEL), BF16)],
        compiler_params=_params(("parallel", "arbitrary")),
        name="ffn_up",
    )(x1, x1, mod, pre_g, w_up, w_up, conv_w, conv_w, conv_b, conv_b)


def _ffn_down_kernel(act_ref, x_ref, mod_ref, pg_ref, wd_ref, out_ref):
    y = jnp.dot(act_ref[...], wd_ref[...], preferred_element_type=F32)
    _residual_rms_to(out_ref, x_ref, y, mod_ref[0, 5:6, :] * pg_ref[...])


def _ffn_down(act, x1, mod, post_g, w_down, tm):
    t = x1.shape[0]
    tiles_per_seq = SEQ // tm
    return pl.pallas_call(
        _ffn_down_kernel,
        grid=(t // tm,),
        in_specs=[
            pl.BlockSpec((tm, D_FF), lambda i: (i, 0)),
            pl.BlockSpec((tm, D_MODEL), lambda i: (i, 0)),
            pl.BlockSpec((1, N_MOD, D_MODEL), lambda i: (i // tiles_per_seq, 0, 0)),
            pl.BlockSpec((1, D_MODEL), lambda i: (0, 0)),
            pl.BlockSpec((D_FF, D_MODEL), lambda i: (0, 0), pipeline_mode=pl.Buffered(1)),
        ],
        out_specs=pl.BlockSpec((tm, D_MODEL), lambda i: (i, 0)),
        out_shape=jax.ShapeDtypeStruct((t, D_MODEL), F32),
        compiler_params=_params(("parallel",)),
        name="ffn_down",
    )(act, x1, mod, post_g, w_down)


def _layout_w_in_kernel(w_ref, o_ref):
    q0, kv0, pe0, ga0, gb0 = 4096, 4608, 4864, 4928, 6976
    half = QK_ROPE // 2
    rows = w_ref.shape[0]
    o_ref[:, 0:q0] = w_ref[:, 0:q0].astype(BF16)
    o_ref[:, q0:q0 + 2 * D_MODEL] = w_ref[:, ga0:ga0 + 2 * D_MODEL].astype(BF16)
    lat = q0 + 2 * D_MODEL
    o_ref[:, lat:lat + pe0 - q0] = w_ref[:, q0:pe0].astype(BF16)
    x1 = w_ref[:, pe0:pe0 + half]
    x2 = w_ref[:, pe0 + half:pe0 + QK_ROPE]
    zeros = jnp.zeros((rows, LANES - QK_ROPE), F32)
    o_ref[:, lat + pe0 - q0:] = jnp.concatenate([x1, x2, zeros, x2, x1, zeros], axis=1).astype(BF16)


def _layout_w_in(w_in):
    rows = 256
    return pl.pallas_call(
        _layout_w_in_kernel,
        grid=(D_MODEL // rows,),
        in_specs=[pl.BlockSpec((rows, w_in.shape[1]), lambda i: (i, 0))],
        out_specs=pl.BlockSpec((rows, Z_COLS), lambda i: (i, 0)),
        out_shape=jax.ShapeDtypeStruct((D_MODEL, Z_COLS), BF16),
        compiler_params=_params(("parallel",)),
        name="layout_w_in",
    )(w_in)


def _layout_w_uq(w_uq):
    w = w_uq.reshape(Q_LORA, MLA_HEADS, QK_NOPE + QK_ROPE)
    half = QK_ROPE // 2
    zeros = jnp.zeros((Q_LORA, MLA_HEADS, LANES - QK_ROPE), w.dtype)
    main = jnp.concatenate([w, zeros], axis=-1).reshape(Q_LORA, MLA_HEADS * HEAD_PAD)
    swap = jnp.concatenate([w[:, :, QK_NOPE + half:], w[:, :, QK_NOPE:QK_NOPE + half], zeros], axis=-1)
    return main.astype(BF16), swap.reshape(Q_LORA, MLA_HEADS * LANES).astype(BF16)


def _layout_w_ukv(w_ukv):
    w = w_ukv.reshape(KV_LORA, MLA_HEADS, QK_NOPE + V_HEAD)
    wk = w[:, :, :QK_NOPE].reshape(KV_LORA, MLA_HEADS * QK_NOPE)
    wv = w[:, :, QK_NOPE:].reshape(KV_LORA, MLA_HEADS * V_HEAD)
    return wk.astype(BF16), wv.astype(BF16)


def kernel(x, c, positions, w_ada, b_ada, pre_norm1_g, w_in, gm_ln_g, gm_ln_b, gm_w_s, gm_b_s, w_branch_a, q_norm_g, w_uq, kv_norm_g, w_ukv, w_branch_b, w_out, post_norm1_g, pre_norm2_g, w_up, conv_w, conv_b, w_down, post_norm2_g):
    bsz, seq, d = x.shape
    t = bsz * seq
    x2d = x.reshape(t, d)
    pos2d = positions.reshape(t, 1)
    row = lambda a: a.reshape(1, -1)

    inv = ROPE_THETA ** (-jnp.arange(0, QK_ROPE, 2, dtype=F32) / QK_ROPE)
    pad = jnp.zeros((LANES - QK_ROPE,), F32)
    inv2 = row(jnp.concatenate([inv, inv, pad]))
    sgn2 = row(jnp.concatenate([-jnp.ones_like(inv), jnp.ones_like(inv), pad]))

    w_z = _layout_w_in(w_in)
    wq, wqs = _layout_w_uq(w_uq)
    wk, wv = _layout_w_ukv(w_ukv)
    b_s_bcast = jnp.broadcast_to(gm_b_s[:, :, None], (GM_GROUPS, CHUNK, CHUNK))

    mod = _ada(c, w_ada, b_ada).reshape(bsz, N_MOD, d)
    z = _inproj(x2d, mod, row(pre_norm1_g), w_z, tm=1024)
    yag = _gmlp(z, row(gm_ln_g), row(gm_ln_b), gm_w_s, b_s_bcast, w_branch_a.astype(BF16), tm=512)
    q, k, v = _mla_proj(z, pos2d, inv2, sgn2, row(q_norm_g), row(kv_norm_g), wq, wqs, wk, wv, tm=512)
    o = _attention(q, k, v, bsz, tq=512)
    x1 = _merge(o, yag, z, x2d, mod, row(post_norm1_g), w_branch_b.astype(BF16), w_out.astype(BF16), tm=256)
    act = _ffn_up(x1, mod, row(pre_norm2_g), w_up, conv_w, row(conv_b), tm=1024, tn=512)
    x2 = _ffn_down(act, x1, mod, row(post_norm2_g), w_down.astype(BF16), tm=512)
    return x2.reshape(bsz, seq, d)
```

```python
import functools

import jax
import jax.numpy as jnp
from jax import lax
from jax.experimental import pallas as pl
from jax.experimental.pallas import tpu as pltpu

F32 = jnp.float32
BF16 = jnp.bfloat16

D_MODEL = 2048
SEQ = 2048
GM_WIDTH = 2048
CHUNK = 128
GM_GROUPS = 16
MLA_HEADS = 16
Q_LORA = 512
KV_LORA = 256
QK_NOPE = 128
QK_ROPE = 64
V_HEAD = 128
ROPE_THETA = 10000.0
D_FF = 5632
CONV_W = 3
EPS = 1e-6
N_MOD = 6
LOG2_E = 1.4426950408889634

LANES = 128
HEAD_PAD = 2 * LANES
HALO = 16
VMEM_LIMIT = 56 * 1024 * 1024

Z_TILE = 1024
Z_COLS = 9 * Z_TILE


def _rms(x, g):
    return x * lax.rsqrt(jnp.mean(x * x, axis=-1, keepdims=True) + EPS) * g


def _sigmoid(x):
    return 0.5 + 0.5 * jnp.tanh(0.5 * x)


NORM_ROWS = 128
GMLP_ROWS = 256
MLA_ROWS = 256


def _modulated_rms_to(h_ref, row0, x_ref, gain, shift):
    for r in range(0, x_ref.shape[0], NORM_ROWS):
        n = min(NORM_ROWS, x_ref.shape[0] - r)
        x = x_ref[r:r + n, :]
        inv = lax.rsqrt(jnp.mean(x * x, axis=-1, keepdims=True) + EPS)
        h_ref[row0 + r:row0 + r + n, :] = (x * inv * gain + shift).astype(BF16)


def _residual_rms_to(out_ref, x_ref, y, gain):
    for r in range(0, x_ref.shape[0], NORM_ROWS):
        n = min(NORM_ROWS, x_ref.shape[0] - r)
        yc = y[r:r + n, :]
        inv = lax.rsqrt(jnp.mean(yc * yc, axis=-1, keepdims=True) + EPS)
        out_ref[r:r + n, :] = x_ref[r:r + n, :] + yc * inv * gain


def _params(sem):
    return pltpu.CompilerParams(dimension_semantics=sem, vmem_limit_bytes=VMEM_LIMIT)


def _ada_kernel(c_ref, w_ref, b_ref, o_ref):
    c = c_ref[...]
    sc = (c * jax.nn.sigmoid(c)).astype(BF16)
    o_ref[...] = jnp.dot(sc, w_ref[...].astype(BF16), preferred_element_type=F32) + b_ref[...]


def _ada(c, w_ada, b_ada):
    bsz = c.shape[0]
    n = w_ada.shape[1]
    tn = 1024
    return pl.pallas_call(
        _ada_kernel,
        grid=(n // tn,),
        in_specs=[
            pl.BlockSpec((bsz, D_MODEL), lambda j: (0, 0)),
            pl.BlockSpec((D_MODEL, tn), lambda j: (0, j)),
            pl.BlockSpec((1, tn), lambda j: (0, j)),
        ],
        out_specs=pl.BlockSpec((bsz, tn), lambda j: (0, j)),
        out_shape=jax.ShapeDtypeStruct((bsz, n), F32),
        compiler_params=_params(("arbitrary",)),
        name="ada",
    )(c, w_ada, b_ada.reshape(1, n))


N_Z_TILES = Z_COLS // Z_TILE


def _inproj_kernel(x_ref, mod_ref, g_ref, w_ref, z_ref, h_ref):
    j = pl.program_id(1)

    @pl.when(j == 0)
    def _():
        _modulated_rms_to(h_ref, 0, x_ref, g_ref[...] * (1.0 + mod_ref[0, 1:2, :]), mod_ref[0, 0:1, :])

    acc = lax.dot_general(h_ref[...], w_ref[...], (((1,), (1,)), ((), ())), preferred_element_type=F32)

    @pl.when(j < 4)
    def _():
        c = 0.7978845608028654
        half = 0.5 * acc
        z_ref[...] = (half + half * jnp.tanh(acc * (c + (c * 0.044715) * (acc * acc)))).astype(BF16)

    @pl.when(j >= 4)
    def _():
        z_ref[...] = acc.astype(BF16)


def _inproj(x2d, mod, pre_g, w_zt, tm):
    t = x2d.shape[0]
    tiles_per_seq = SEQ // tm
    return pl.pallas_call(
        _inproj_kernel,
        grid=(t // tm, N_Z_TILES),
        in_specs=[
            pl.BlockSpec((tm, D_MODEL), lambda i, j: (i, 0)),
            pl.BlockSpec((1, N_MOD, D_MODEL), lambda i, j: (i // tiles_per_seq, 0, 0)),
            pl.BlockSpec((1, D_MODEL), lambda i, j: (0, 0)),
            pl.BlockSpec((Z_TILE, D_MODEL), lambda i, j: (j, 0)),
        ],
        out_specs=pl.BlockSpec((tm, Z_TILE), lambda i, j: (i, j)),
        out_shape=jax.ShapeDtypeStruct((t, Z_COLS), BF16),
        scratch_shapes=[pltpu.VMEM((tm, D_MODEL), BF16)],
        compiler_params=_params(("parallel", "arbitrary")),
        name="inproj",
    )(x2d, mod, pre_g, w_zt)


def _gmlp_kernel(u_ref, v_ref, ga_ref, lng_ref, lnb_ref, ws_ref, bs_ref, wa32_ref, o_ref, vln_ref, a_ref, wa_ref):
    tm = u_ref.shape[0]

    @pl.when(pl.program_id(0) == 0)
    def _():
        wa_ref[...] = wa32_ref[...].astype(BF16)

    row = lax.broadcasted_iota(jnp.int32, (CHUNK, CHUNK), 0)
    col = lax.broadcasted_iota(jnp.int32, (CHUNK, CHUNK), 1)
    mask = (col <= row).astype(F32)
    wsm = [(ws_ref[g] * mask).astype(BF16) for g in range(GM_GROUPS)]
    for r0 in range(0, tm, GMLP_ROWS):
        for c0 in range(r0, r0 + GMLP_ROWS, CHUNK):
            rs = slice(c0, c0 + CHUNK)
            v = v_ref[rs, :].astype(F32)
            mu = jnp.mean(v, axis=-1, keepdims=True)
            var = jnp.mean(jnp.square(v - mu), axis=-1, keepdims=True)
            vln_ref[rs, :] = ((v - mu) * lax.rsqrt(var + EPS) * lng_ref[...] + lnb_ref[...]).astype(BF16)
            for g in range(GM_GROUPS):
                cs = slice(g * CHUNK, (g + 1) * CHUNK)
                mixed = jnp.dot(wsm[g], vln_ref[rs, cs], preferred_element_type=F32) + bs_ref[g]
                a_ref[rs, cs] = (u_ref[rs, cs].astype(F32) * mixed).astype(BF16)
        rg = slice(r0, r0 + GMLP_ROWS)
        ya = jnp.dot(a_ref[rg, :], wa_ref[...], preferred_element_type=F32)
        o_ref[rg, :] = (_sigmoid(ga_ref[rg, :].astype(F32)) * ya).astype(BF16)


def _gmlp(z, ln_g, ln_b, w_s, b_s_bcast, w_a, tm):
    t = z.shape[0]
    return pl.pallas_call(
        _gmlp_kernel,
        grid=(t // tm,),
        in_specs=[
            pl.BlockSpec((tm, GM_WIDTH), lambda i: (i, 0)),
            pl.BlockSpec((tm, GM_WIDTH), lambda i: (i, 1)),
            pl.BlockSpec((tm, D_MODEL), lambda i: (i, 2)),
            pl.BlockSpec((1, GM_WIDTH), lambda i: (0, 0)),
            pl.BlockSpec((1, GM_WIDTH), lambda i: (0, 0)),
            pl.BlockSpec((GM_GROUPS, CHUNK, CHUNK), lambda i: (0, 0, 0)),
            pl.BlockSpec((GM_GROUPS, CHUNK, CHUNK), lambda i: (0, 0, 0)),
            pl.BlockSpec((GM_WIDTH, D_MODEL), lambda i: (0, 0), pipeline_mode=pl.Buffered(1)),
        ],
        out_specs=pl.BlockSpec((tm, D_MODEL), lambda i: (i, 0)),
        out_shape=jax.ShapeDtypeStruct((t, D_MODEL), BF16),
        scratch_shapes=[pltpu.VMEM((tm, GM_WIDTH), BF16), pltpu.VMEM((tm, GM_WIDTH), BF16),
                        pltpu.VMEM((GM_WIDTH, D_MODEL), BF16)],
        compiler_params=_params(("arbitrary",)),
        name="gmlp",
    )(z, z, z, ln_g, ln_b, w_s, b_s_bcast, w_a)


def _mla_proj_kernel(zl_ref, pos_ref, inv_ref, sgn_ref, qg_ref, kvg_ref, wq_ref, wqs_ref, wk_ref, wv_ref,
                     q_ref, k_ref, v_ref):
    scale = LOG2_E * (QK_NOPE + QK_ROPE) ** -0.5
    lat = Q_LORA + KV_LORA
    for r0 in range(0, zl_ref.shape[0], MLA_ROWS):
        rs = slice(r0, r0 + MLA_ROWS)
        ang = pos_ref[rs, :].astype(F32) * inv_ref[...]
        cos_t = jnp.cos(ang)
        sin_t = jnp.sin(ang) * sgn_ref[...]
        ql = _rms(zl_ref[rs, 0:Q_LORA].astype(F32), qg_ref[...]).astype(BF16)
        kvl = _rms(zl_ref[rs, Q_LORA:lat].astype(F32), kvg_ref[...]).astype(BF16)
        kpe = (zl_ref[rs, lat:lat + LANES].astype(F32) * cos_t
               + zl_ref[rs, lat + LANES:lat + 2 * LANES].astype(F32) * sin_t).astype(BF16)
        cos_t = cos_t * scale
        sin_t = sin_t * scale
        qm = jnp.dot(ql, wq_ref[...], preferred_element_type=F32)
        qs = jnp.dot(ql, wqs_ref[...], preferred_element_type=F32)
        kn = jnp.dot(kvl, wk_ref[...], preferred_element_type=F32)
        v_ref[rs, :] = jnp.dot(kvl, wv_ref[...], preferred_element_type=F32).astype(BF16)
        for h in range(MLA_HEADS):
            o = h * HEAD_PAD
            q_ref[rs, o:o + LANES] = (qm[:, o:o + LANES] * scale).astype(BF16)
            q_ref[rs, o + LANES:o + HEAD_PAD] = (
                qm[:, o + LANES:o + HEAD_PAD] * cos_t + qs[:, h * LANES:(h + 1) * LANES] * sin_t).astype(BF16)
            k_ref[rs, o:o + LANES] = kn[:, h * LANES:(h + 1) * LANES].astype(BF16)
            k_ref[rs, o + LANES:o + HEAD_PAD] = kpe


def _mla_proj(z, pos2d, inv2, sgn2, q_g, kv_g, wq, wqs, wk, wv, tm):
    t = z.shape[0]
    hw = MLA_HEADS * HEAD_PAD
    const = lambda i: (0, 0)
    return pl.pallas_call(
        _mla_proj_kernel,
        grid=(t // tm,),
        in_specs=[
            pl.BlockSpec((tm, Z_TILE), lambda i: (i, 8)),
            pl.BlockSpec((tm, 1), lambda i: (i, 0)),
            pl.BlockSpec((1, LANES), const),
            pl.BlockSpec((1, LANES), const),
            pl.BlockSpec((1, Q_LORA), const),
            pl.BlockSpec((1, KV_LORA), const),
            pl.BlockSpec(wq.shape, const),
            pl.BlockSpec(wqs.shape, const),
            pl.BlockSpec(wk.shape, const),
            pl.BlockSpec(wv.shape, const),
        ],
        out_specs=[
            pl.BlockSpec((tm, hw), lambda i: (i, 0)),
            pl.BlockSpec((tm, hw), lambda i: (i, 0)),
            pl.BlockSpec((tm, MLA_HEADS * V_HEAD), lambda i: (i, 0)),
        ],
        out_shape=[
            jax.ShapeDtypeStruct((t, hw), BF16),
            jax.ShapeDtypeStruct((t, hw), BF16),
            jax.ShapeDtypeStruct((t, MLA_HEADS * V_HEAD), BF16),
        ],
        compiler_params=_params(("parallel",)),
        name="mla_proj",
    )(z, pos2d, inv2, sgn2, q_g, kv_g, wq, wqs, wk, wv)


def _attn_kernel(q_ref, k_ref, v_ref, o_ref, *, tq, heads):
    nt = (((1,), (1,)), ((), ()))
    row = lax.broadcasted_iota(jnp.int32, (tq, tq), 0)
    col = lax.broadcasted_iota(jnp.int32, (tq, tq), 1)
    for h in range(heads):
        for i in reversed(range(SEQ // tq)):
            lo = i * tq
            qk = slice(h * HEAD_PAD, (h + 1) * HEAD_PAD)
            vo = slice(h * V_HEAD, (h + 1) * V_HEAD)
            q = q_ref[lo:lo + tq, qk]
            s_d = lax.dot_general(q, k_ref[lo:lo + tq, qk], nt, preferred_element_type=F32)
            s_d = jnp.where(col <= row, s_d, -1e30)
            m = jnp.max(s_d, axis=-1, keepdims=True)
            if i > 0:
                s_o = lax.dot_general(q, k_ref[0:lo, qk], nt, preferred_element_type=F32)
                m = jnp.maximum(m, jnp.max(s_o, axis=-1, keepdims=True))
            p_d = jnp.exp2(s_d - m)
            l = jnp.sum(p_d, axis=-1, keepdims=True)
            acc = jnp.dot(p_d.astype(BF16), v_ref[lo:lo + tq, vo], preferred_element_type=F32)
            if i > 0:
                p_o = jnp.exp2(s_o - m)
                l = l + jnp.sum(p_o, axis=-1, keepdims=True)
                acc = acc + jnp.dot(p_o.astype(BF16), v_ref[0:lo, vo], preferred_element_type=F32)
            o_ref[lo:lo + tq, vo] = (acc / l).astype(BF16)


def _attention(q, k, v, bsz, tq, heads):
    t = q.shape[0]
    return pl.pallas_call(
        functools.partial(_attn_kernel, tq=tq, heads=heads),
        grid=(bsz, MLA_HEADS // heads),
        in_specs=[
            pl.BlockSpec((SEQ, heads * HEAD_PAD), lambda b, h: (b, h)),
            pl.BlockSpec((SEQ, heads * HEAD_PAD), lambda b, h: (b, h)),
            pl.BlockSpec((SEQ, heads * V_HEAD), lambda b, h: (b, h)),
        ],
        out_specs=pl.BlockSpec((SEQ, heads * V_HEAD), lambda b, h: (b, h)),
        out_shape=jax.ShapeDtypeStruct((t, MLA_HEADS * V_HEAD), BF16),
        compiler_params=_params(("parallel", "parallel")),
        name="attn",
    )(q, k, v)


def _merge_kernel(o_ref, yag_ref, gb_ref, x_ref, mod_ref, pg_ref, wb_ref, wo_ref, out_ref):
    yb = jnp.dot(o_ref[...], wb_ref[...], preferred_element_type=F32)
    merged = yag_ref[...].astype(F32) + _sigmoid(gb_ref[...].astype(F32)) * yb
    y = jnp.dot(merged.astype(BF16), wo_ref[...], preferred_element_type=F32)
    _residual_rms_to(out_ref, x_ref, y, mod_ref[0, 2:3, :] * pg_ref[...])


def _merge(o, yag, z, x2d, mod, post_g, wb, wo, tm):
    t = x2d.shape[0]
    tiles_per_seq = SEQ // tm
    const = lambda i: (0, 0)
    return pl.pallas_call(
        _merge_kernel,
        grid=(t // tm,),
        in_specs=[
            pl.BlockSpec((tm, D_MODEL), lambda i: (i, 0)),
            pl.BlockSpec((tm, D_MODEL), lambda i: (i, 0)),
            pl.BlockSpec((tm, D_MODEL), lambda i: (i, 3)),
            pl.BlockSpec((tm, D_MODEL), lambda i: (i, 0)),
            pl.BlockSpec((1, N_MOD, D_MODEL), lambda i: (i // tiles_per_seq, 0, 0)),
            pl.BlockSpec((1, D_MODEL), const),
            pl.BlockSpec(wb.shape, const, pipeline_mode=pl.Buffered(1)),
            pl.BlockSpec(wo.shape, const, pipeline_mode=pl.Buffered(1)),
        ],
        out_specs=pl.BlockSpec((tm, D_MODEL), lambda i: (i, 0)),
        out_shape=jax.ShapeDtypeStruct((t, D_MODEL), F32),
        compiler_params=_params(("parallel",)),
        name="merge",
    )(o, yag, z, x2d, mod, post_g, wb, wo)


def _ffn_up_kernel(x_ref, halo_ref, mod_ref, g_ref, wg_ref, wv_ref, cwg_ref, cwv_ref, cbg_ref, cbv_ref,
                   act_ref, h_ref, *, tiles_per_seq):
    tm = x_ref.shape[0]
    i = pl.program_id(0)

    @pl.when(pl.program_id(1) == 0)
    def _():
        gain = g_ref[...] * (1.0 + mod_ref[0, 4:5, :])
        shift = mod_ref[0, 3:4, :]
        _modulated_rms_to(h_ref, HALO, x_ref, gain, shift)
        keep = (i % tiles_per_seq != 0).astype(F32)
        _modulated_rms_to(h_ref, 0, halo_ref, gain * keep, shift * keep)

    hext = h_ref[...]

    def conv(w_ref, cw_ref, cb_ref):
        up = jnp.dot(hext, w_ref[...].astype(BF16), preferred_element_type=F32)
        y = cw_ref[0:1, :] * pltpu.roll(up, 2, axis=0)[HALO:HALO + tm, :]
        y = y + cw_ref[1:2, :] * pltpu.roll(up, 1, axis=0)[HALO:HALO + tm, :]
        y = y + cw_ref[2:3, :] * up[HALO:HALO + tm, :]
        return y + cb_ref[...]

    gate = conv(wg_ref, cwg_ref, cbg_ref)
    val = conv(wv_ref, cwv_ref, cbv_ref)
    act_ref[...] = (gate * jax.nn.sigmoid(gate) * val).astype(BF16)


def _ffn_up(x1, mod, pre_g, w_up, conv_w, conv_b, tm, tn):
    t = x1.shape[0]
    tiles_per_seq = SEQ // tm
    nj = D_FF // tn
    halo_blocks = tm // HALO
    return pl.pallas_call(
        functools.partial(_ffn_up_kernel, tiles_per_seq=tiles_per_seq),
        grid=(t // tm, nj),
        in_specs=[
            pl.BlockSpec((tm, D_MODEL), lambda i, j: (i, 0)),
            pl.BlockSpec((HALO, D_MODEL), lambda i, j: (jnp.maximum(i * halo_blocks - 1, 0), 0)),
            pl.BlockSpec((1, N_MOD, D_MODEL), lambda i, j: (i // tiles_per_seq, 0, 0)),
            pl.BlockSpec((1, D_MODEL), lambda i, j: (0, 0)),
            pl.BlockSpec((D_MODEL, tn), lambda i, j: (0, j)),
            pl.BlockSpec((D_MODEL, tn), lambda i, j: (0, nj + j)),
            pl.BlockSpec((CONV_W, tn), lambda i, j: (0, j)),
            pl.BlockSpec((CONV_W, tn), lambda i, j: (0, nj + j)),
            pl.BlockSpec((1, tn), lambda i, j: (0, j)),
            pl.BlockSpec((1, tn), lambda i, j: (0, nj + j)),
        ],
        out_specs=pl.BlockSpec((tm, tn), lambda i, j: (i, j)),
        out_shape=jax.ShapeDtypeStruct((t, D_FF), BF16),
        scratch_shapes=[pltpu.VMEM((HALO + tm, D_MODEL), BF16)],
        compiler_params=_params(("parallel", "arbitrary")),
        name="ffn_up",
    )(x1, x1, mod, pre_g, w_up, w_up, conv_w, conv_w, conv_b, conv_b)


def _ffn_down_kernel(act_ref, x_ref, mod_ref, pg_ref, wd_ref, out_ref):
    y = jnp.dot(act_ref[...], wd_ref[...], preferred_element_type=F32)
    _residual_rms_to(out_ref, x_ref, y, mod_ref[0, 5:6, :] * pg_ref[...])


def _ffn_down(act, x1, mod, post_g, w_down, tm):
    t = x1.shape[0]
    tiles_per_seq = SEQ // tm
    return pl.pallas_call(
        _ffn_down_kernel,
        grid=(t // tm,),
        in_specs=[
            pl.BlockSpec((tm, D_FF), lambda i: (i, 0)),
            pl.BlockSpec((tm, D_MODEL), lambda i: (i, 0)),
            pl.BlockSpec((1, N_MOD, D_MODEL), lambda i: (i // tiles_per_seq, 0, 0)),
            pl.BlockSpec((1, D_MODEL), lambda i: (0, 0)),
            pl.BlockSpec((D_FF, D_MODEL), lambda i: (0, 0), pipeline_mode=pl.Buffered(1)),
        ],
        out_specs=pl.BlockSpec((tm, D_MODEL), lambda i: (i, 0)),
        out_shape=jax.ShapeDtypeStruct((t, D_MODEL), F32),
        compiler_params=_params(("parallel",)),
        name="ffn_down",
    )(act, x1, mod, post_g, w_down)


W_IN_Q0, W_IN_PE0, W_IN_GA0 = 4096, 4864, 4928


def _layout_w_in_kernel(w_ref, o_ref):
    j = pl.program_id(0)

    @pl.when(j < N_Z_TILES - 1)
    def _():
        o_ref[...] = w_ref[...].astype(BF16)

    @pl.when(j == N_Z_TILES - 1)
    def _():
        n_lat = W_IN_PE0 - W_IN_Q0
        half = QK_ROPE // 2
        o_ref[0:n_lat + QK_ROPE, :] = w_ref[0:n_lat + QK_ROPE, :].astype(BF16)
        zeros = jnp.zeros((LANES - QK_ROPE, D_MODEL), BF16)
        o_ref[n_lat + QK_ROPE:n_lat + LANES, :] = zeros
        o_ref[n_lat + LANES:n_lat + LANES + half, :] = w_ref[n_lat + half:n_lat + QK_ROPE, :].astype(BF16)
        o_ref[n_lat + LANES + half:n_lat + LANES + QK_ROPE, :] = w_ref[n_lat:n_lat + half, :].astype(BF16)
        o_ref[n_lat + LANES + QK_ROPE:, :] = zeros


def _layout_w_in(w_in_t):
    g = 64
    def src_row(j):
        blk = jnp.where(j < 4, j * (Z_TILE // g),
                        jnp.where(j < 8, W_IN_GA0 // g + (j - 4) * (Z_TILE // g), W_IN_Q0 // g))
        return blk * g
    return pl.pallas_call(
        _layout_w_in_kernel,
        grid=(N_Z_TILES,),
        in_specs=[pl.BlockSpec((pl.Element(Z_TILE), pl.Element(D_MODEL)), lambda j: (src_row(j), 0))],
        out_specs=pl.BlockSpec((Z_TILE, D_MODEL), lambda j: (j, 0)),
        out_shape=jax.ShapeDtypeStruct((Z_COLS, D_MODEL), BF16),
        compiler_params=_params(("parallel",)),
        name="layout_w_in",
    )(w_in_t)


def _layout_w_uq(w_uq):
    w = w_uq.reshape(Q_LORA, MLA_HEADS, QK_NOPE + QK_ROPE)
    half = QK_ROPE // 2
    zeros = jnp.zeros((Q_LORA, MLA_HEADS, LANES - QK_ROPE), w.dtype)
    main = jnp.concatenate([w, zeros], axis=-1).reshape(Q_LORA, MLA_HEADS * HEAD_PAD)
    swap = jnp.concatenate([w[:, :, QK_NOPE + half:], w[:, :, QK_NOPE:QK_NOPE + half], zeros], axis=-1)
    return main.astype(BF16), swap.reshape(Q_LORA, MLA_HEADS * LANES).astype(BF16)


def _layout_w_ukv(w_ukv):
    w = w_ukv.reshape(KV_LORA, MLA_HEADS, QK_NOPE + V_HEAD)
    wk = w[:, :, :QK_NOPE].reshape(KV_LORA, MLA_HEADS * QK_NOPE)
    wv = w[:, :, QK_NOPE:].reshape(KV_LORA, MLA_HEADS * V_HEAD)
    return wk.astype(BF16), wv.astype(BF16)


def kernel(x, c, positions, w_ada, b_ada, pre_norm1_g, w_in, gm_ln_g, gm_ln_b, gm_w_s, gm_b_s, w_branch_a, q_norm_g, w_uq, kv_norm_g, w_ukv, w_branch_b, w_out, post_norm1_g, pre_norm2_g, w_up, conv_w, conv_b, w_down, post_norm2_g):
    bsz, seq, d = x.shape
    t = bsz * seq
    x2d = x.reshape(t, d)
    pos2d = positions.reshape(t, 1)
    row = lambda a: a.reshape(1, -1)

    inv = ROPE_THETA ** (-jnp.arange(0, QK_ROPE, 2, dtype=F32) / QK_ROPE)
    pad = jnp.zeros((LANES - QK_ROPE,), F32)
    inv2 = row(jnp.concatenate([inv, inv, pad]))
    sgn2 = row(jnp.concatenate([-jnp.ones_like(inv), jnp.ones_like(inv), pad]))

    w_zt = _layout_w_in(w_in.T)
    wq, wqs = _layout_w_uq(w_uq)
    wk, wv = _layout_w_ukv(w_ukv)
    b_s_bcast = jnp.broadcast_to(gm_b_s[:, :, None], (GM_GROUPS, CHUNK, CHUNK))

    mod = _ada(c, w_ada, b_ada).reshape(bsz, N_MOD, d)
    z = _inproj(x2d, mod, row(pre_norm1_g), w_zt, tm=1024)
    yag = _gmlp(z, row(gm_ln_g), row(gm_ln_b), gm_w_s, b_s_bcast, w_branch_a, tm=512)
    q, k, v = _mla_proj(z, pos2d, inv2, sgn2, row(q_norm_g), row(kv_norm_g), wq, wqs, wk, wv, tm=512)
    o = _attention(q, k, v, bsz, tq=512, heads=4)
    x1 = _merge(o, yag, z, x2d, mod, row(post_norm1_g), w_branch_b.astype(BF16), w_out.astype(BF16), tm=512)
    act = _ffn_up(x1, mod, row(pre_norm2_g), w_up, conv_w, row(conv_b), tm=1024, tn=512)
    x2 = _ffn_down(act, x1, mod, row(post_norm2_g), w_down.astype(BF16), tm=512)
    return x2.reshape(bsz, seq, d)
```

```python
import functools

import jax
import jax.numpy as jnp
from jax import lax
from jax.experimental import pallas as pl
from jax.experimental.pallas import tpu as pltpu

F32 = jnp.float32
BF16 = jnp.bfloat16

D_MODEL = 2048
SEQ = 2048
GM_WIDTH = 2048
CHUNK = 128
GM_GROUPS = 16
MLA_HEADS = 16
Q_LORA = 512
KV_LORA = 256
QK_NOPE = 128
QK_ROPE = 64
V_HEAD = 128
ROPE_THETA = 10000.0
D_FF = 5632
CONV_W = 3
EPS = 1e-6
N_MOD = 6
LOG2_E = 1.4426950408889634

LANES = 128
HEAD_PAD = 2 * LANES
HALO = 16
VMEM_LIMIT = 56 * 1024 * 1024

Z_TILE = 1024
Z_COLS = 9 * Z_TILE


def _rms(x, g):
    return x * lax.rsqrt(jnp.mean(x * x, axis=-1, keepdims=True) + EPS) * g


def _sigmoid(x):
    return 0.5 + 0.5 * jnp.tanh(0.5 * x)


NORM_ROWS = 128
GMLP_ROWS = 256
MLA_ROWS = 256


def _modulated_rms_to(h_ref, row0, x_ref, gain, shift):
    for r in range(0, x_ref.shape[0], NORM_ROWS):
        n = min(NORM_ROWS, x_ref.shape[0] - r)
        x = x_ref[r:r + n, :]
        inv = lax.rsqrt(jnp.mean(x * x, axis=-1, keepdims=True) + EPS)
        h_ref[row0 + r:row0 + r + n, :] = (x * inv * gain + shift).astype(BF16)


def _residual_rms_to(out_ref, x_ref, y, gain):
    for r in range(0, x_ref.shape[0], NORM_ROWS):
        n = min(NORM_ROWS, x_ref.shape[0] - r)
        yc = y[r:r + n, :]
        inv = lax.rsqrt(jnp.mean(yc * yc, axis=-1, keepdims=True) + EPS)
        out_ref[r:r + n, :] = x_ref[r:r + n, :] + yc * inv * gain


def _params(sem):
    return pltpu.CompilerParams(dimension_semantics=sem, vmem_limit_bytes=VMEM_LIMIT)


def _ada_kernel(c_ref, w_ref, b_ref, o_ref):
    c = c_ref[...]
    sc = (c * jax.nn.sigmoid(c)).astype(BF16)
    o_ref[...] = jnp.dot(sc, w_ref[...].astype(BF16), preferred_element_type=F32) + b_ref[...]


def _ada(c, w_ada, b_ada):
    bsz = c.shape[0]
    n = w_ada.shape[1]
    tn = 1024
    return pl.pallas_call(
        _ada_kernel,
        grid=(n // tn,),
        in_specs=[
            pl.BlockSpec((bsz, D_MODEL), lambda j: (0, 0)),
            pl.BlockSpec((D_MODEL, tn), lambda j: (0, j)),
            pl.BlockSpec((1, tn), lambda j: (0, j)),
        ],
        out_specs=pl.BlockSpec((bsz, tn), lambda j: (0, j)),
        out_shape=jax.ShapeDtypeStruct((bsz, n), F32),
        compiler_params=_params(("arbitrary",)),
        name="ada",
    )(c, w_ada, b_ada.reshape(1, n))


N_Z_TILES = Z_COLS // Z_TILE
INPROJ_TILE = 1536
GELU_COLS = 2 * GM_WIDTH


def _gelu_tanh(x):
    c = 0.7978845608028654
    half = 0.5 * x
    return half + half * jnp.tanh(x * (c + (c * 0.044715) * (x * x)))


def _inproj_kernel(x_ref, mod_ref, g_ref, w_ref, z_ref, h_ref):
    j = pl.program_id(1)

    @pl.when(j == 0)
    def _():
        _modulated_rms_to(h_ref, 0, x_ref, g_ref[...] * (1.0 + mod_ref[0, 1:2, :]), mod_ref[0, 0:1, :])

    acc = lax.dot_general(h_ref[...], w_ref[...], (((1,), (1,)), ((), ())), preferred_element_type=F32)
    n_gelu, mixed_cols = divmod(GELU_COLS, INPROJ_TILE)

    @pl.when(j < n_gelu)
    def _():
        z_ref[...] = _gelu_tanh(acc).astype(BF16)

    @pl.when(j == n_gelu)
    def _():
        z_ref[:, :mixed_cols] = _gelu_tanh(acc[:, :mixed_cols]).astype(BF16)
        z_ref[:, mixed_cols:] = acc[:, mixed_cols:].astype(BF16)

    @pl.when(j > n_gelu)
    def _():
        z_ref[...] = acc.astype(BF16)


def _inproj(x2d, mod, pre_g, w_zt, tm):
    t = x2d.shape[0]
    tiles_per_seq = SEQ // tm
    return pl.pallas_call(
        _inproj_kernel,
        grid=(t // tm, Z_COLS // INPROJ_TILE),
        in_specs=[
            pl.BlockSpec((tm, D_MODEL), lambda i, j: (i, 0)),
            pl.BlockSpec((1, N_MOD, D_MODEL), lambda i, j: (i // tiles_per_seq, 0, 0)),
            pl.BlockSpec((1, D_MODEL), lambda i, j: (0, 0)),
            pl.BlockSpec((INPROJ_TILE, D_MODEL), lambda i, j: (j, 0)),
        ],
        out_specs=pl.BlockSpec((tm, INPROJ_TILE), lambda i, j: (i, j)),
        out_shape=jax.ShapeDtypeStruct((t, Z_COLS), BF16),
        scratch_shapes=[pltpu.VMEM((tm, D_MODEL), BF16)],
        compiler_params=_params(("parallel", "arbitrary")),
        name="inproj",
    )(x2d, mod, pre_g, w_zt)


def _gmlp_kernel(u_ref, v_ref, ga_ref, lng_ref, lnb_ref, ws_ref, bs_ref, wa32_ref, o_ref, vln_ref, a_ref, wa_ref):
    tm = u_ref.shape[0]

    @pl.when(pl.program_id(0) == 0)
    def _():
        wa_ref[...] = wa32_ref[...].astype(BF16)

    row = lax.broadcasted_iota(jnp.int32, (CHUNK, CHUNK), 0)
    col = lax.broadcasted_iota(jnp.int32, (CHUNK, CHUNK), 1)
    mask = (col <= row).astype(F32)
    wsm = [(ws_ref[g] * mask).astype(BF16) for g in range(GM_GROUPS)]
    for r0 in range(0, tm, GMLP_ROWS):
        for c0 in range(r0, r0 + GMLP_ROWS, CHUNK):
            rs = slice(c0, c0 + CHUNK)
            v = v_ref[rs, :].astype(F32)
            mu = jnp.mean(v, axis=-1, keepdims=True)
            var = jnp.mean(jnp.square(v - mu), axis=-1, keepdims=True)
            vln_ref[rs, :] = ((v - mu) * lax.rsqrt(var + EPS) * lng_ref[...] + lnb_ref[...]).astype(BF16)
            for g in range(GM_GROUPS):
                cs = slice(g * CHUNK, (g + 1) * CHUNK)
                mixed = jnp.dot(wsm[g], vln_ref[rs, cs], preferred_element_type=F32) + bs_ref[g]
                a_ref[rs, cs] = (u_ref[rs, cs].astype(F32) * mixed).astype(BF16)
        rg = slice(r0, r0 + GMLP_ROWS)
        ya = jnp.dot(a_ref[rg, :], wa_ref[...], preferred_element_type=F32)
        o_ref[rg, :] = (_sigmoid(ga_ref[rg, :].astype(F32)) * ya).astype(BF16)


def _gmlp(z, ln_g, ln_b, w_s, b_s_bcast, w_a, tm):
    t = z.shape[0]
    return pl.pallas_call(
        _gmlp_kernel,
        grid=(t // tm,),
        in_specs=[
            pl.BlockSpec((tm, GM_WIDTH), lambda i: (i, 0)),
            pl.BlockSpec((tm, GM_WIDTH), lambda i: (i, 1)),
            pl.BlockSpec((tm, D_MODEL), lambda i: (i, 2)),
            pl.BlockSpec((1, GM_WIDTH), lambda i: (0, 0)),
            pl.BlockSpec((1, GM_WIDTH), lambda i: (0, 0)),
            pl.BlockSpec((GM_GROUPS, CHUNK, CHUNK), lambda i: (0, 0, 0)),
            pl.BlockSpec((GM_GROUPS, CHUNK, CHUNK), lambda i: (0, 0, 0)),
            pl.BlockSpec((GM_WIDTH, D_MODEL), lambda i: (0, 0), pipeline_mode=pl.Buffered(1)),
        ],
        out_specs=pl.BlockSpec((tm, D_MODEL), lambda i: (i, 0)),
        out_shape=jax.ShapeDtypeStruct((t, D_MODEL), BF16),
        scratch_shapes=[pltpu.VMEM((tm, GM_WIDTH), BF16), pltpu.VMEM((tm, GM_WIDTH), BF16),
                        pltpu.VMEM((GM_WIDTH, D_MODEL), BF16)],
        compiler_params=_params(("arbitrary",)),
        name="gmlp",
    )(z, z, z, ln_g, ln_b, w_s, b_s_bcast, w_a)


def _rope_kernel(pos_ref, inv_ref, cos_ref, sin_ref):
    ang = pos_ref[...].astype(F32) * inv_ref[...]
    cos_ref[...] = jnp.cos(ang)
    sin_ref[...] = jnp.sin(ang)


def _rope_tables(positions):
    t = positions.size
    half = QK_ROPE // 2
    per_row = LANES // half
    inv = ROPE_THETA ** (-jnp.arange(0, QK_ROPE, 2, dtype=F32) / QK_ROPE)
    pos_d = jnp.repeat(positions.reshape(t // per_row, per_row), half, axis=1)
    rows = t // per_row
    tr = 1024
    cos_d, sin_d = pl.pallas_call(
        _rope_kernel,
        grid=(rows // tr,),
        in_specs=[pl.BlockSpec((tr, LANES), lambda i: (i, 0)), pl.BlockSpec((1, LANES), lambda i: (0, 0))],
        out_specs=[pl.BlockSpec((tr, LANES), lambda i: (i, 0))] * 2,
        out_shape=[jax.ShapeDtypeStruct((rows, LANES), F32)] * 2,
        compiler_params=_params(("parallel",)),
        name="rope_tables",
    )(pos_d, jnp.tile(inv, per_row).reshape(1, LANES))
    return cos_d.reshape(t, half), sin_d.reshape(t, half)


def _mla_proj_kernel(zl_ref, cos_ref, sin_ref, qg_ref, kvg_ref, wq_ref, wqs_ref, wk_ref, wv_ref,
                     q_ref, k_ref, v_ref):
    scale = LOG2_E * (QK_NOPE + QK_ROPE) ** -0.5
    lat = Q_LORA + KV_LORA
    for r0 in range(0, zl_ref.shape[0], MLA_ROWS):
        rs = slice(r0, r0 + MLA_ROWS)
        cos_t = cos_ref[rs, :]
        sin_t = sin_ref[rs, :]
        ql = _rms(zl_ref[rs, 0:Q_LORA].astype(F32), qg_ref[...]).astype(BF16)
        kvl = _rms(zl_ref[rs, Q_LORA:lat].astype(F32), kvg_ref[...]).astype(BF16)
        kpe = (zl_ref[rs, lat:lat + LANES].astype(F32) * cos_t
               + zl_ref[rs, lat + LANES:lat + 2 * LANES].astype(F32) * sin_t).astype(BF16)
        cos_t = cos_t * scale
        sin_t = sin_t * scale
        qm = jnp.dot(ql, wq_ref[...], preferred_element_type=F32)
        qs = jnp.dot(ql, wqs_ref[...], preferred_element_type=F32)
        kn = jnp.dot(kvl, wk_ref[...], preferred_element_type=F32)
        v_ref[rs, :] = jnp.dot(kvl, wv_ref[...], preferred_element_type=F32).astype(BF16)
        for h in range(MLA_HEADS):
            o = h * HEAD_PAD
            q_ref[rs, o:o + LANES] = (qm[:, o:o + LANES] * scale).astype(BF16)
            q_ref[rs, o + LANES:o + HEAD_PAD] = (
                qm[:, o + LANES:o + HEAD_PAD] * cos_t + qs[:, h * LANES:(h + 1) * LANES] * sin_t).astype(BF16)
            k_ref[rs, o:o + LANES] = kn[:, h * LANES:(h + 1) * LANES].astype(BF16)
            k_ref[rs, o + LANES:o + HEAD_PAD] = kpe


def _mla_proj(z, cos_t, sin_t, q_g, kv_g, wq, wqs, wk, wv, tm):
    t = z.shape[0]
    hw = MLA_HEADS * HEAD_PAD
    const = lambda i: (0, 0)
    return pl.pallas_call(
        _mla_proj_kernel,
        grid=(t // tm,),
        in_specs=[
            pl.BlockSpec((tm, Z_TILE), lambda i: (i, 8)),
            pl.BlockSpec((tm, LANES), lambda i: (i, 0)),
            pl.BlockSpec((tm, LANES), lambda i: (i, 0)),
            pl.BlockSpec((1, Q_LORA), const),
            pl.BlockSpec((1, KV_LORA), const),
            pl.BlockSpec(wq.shape, const),
            pl.BlockSpec(wqs.shape, const),
            pl.BlockSpec(wk.shape, const),
            pl.BlockSpec(wv.shape, const),
        ],
        out_specs=[
            pl.BlockSpec((tm, hw), lambda i: (i, 0)),
            pl.BlockSpec((tm, hw), lambda i: (i, 0)),
            pl.BlockSpec((tm, MLA_HEADS * V_HEAD), lambda i: (i, 0)),
        ],
        out_shape=[
            jax.ShapeDtypeStruct((t, hw), BF16),
            jax.ShapeDtypeStruct((t, hw), BF16),
            jax.ShapeDtypeStruct((t, MLA_HEADS * V_HEAD), BF16),
        ],
        compiler_params=_params(("parallel",)),
        name="mla_proj",
    )(z, cos_t, sin_t, q_g, kv_g, wq, wqs, wk, wv)


def _attn_kernel(q_ref, k_ref, v_ref, o_ref, *, tq, heads):
    nt = (((1,), (1,)), ((), ()))
    row = lax.broadcasted_iota(jnp.int32, (tq, tq), 0)
    col = lax.broadcasted_iota(jnp.int32, (tq, tq), 1)
    for h in range(heads):
        for i in reversed(range(SEQ // tq)):
            lo = i * tq
            qk = slice(h * HEAD_PAD, (h + 1) * HEAD_PAD)
            vo = slice(h * V_HEAD, (h + 1) * V_HEAD)
            q = q_ref[lo:lo + tq, qk]
            s_d = lax.dot_general(q, k_ref[lo:lo + tq, qk], nt, preferred_element_type=F32)
            s_d = jnp.where(col <= row, s_d, -1e30)
            m = jnp.max(s_d, axis=-1, keepdims=True)
            if i > 0:
                s_o = lax.dot_general(q, k_ref[0:lo, qk], nt, preferred_element_type=F32)
                m = jnp.maximum(m, jnp.max(s_o, axis=-1, keepdims=True))
            p_d = jnp.exp2(s_d - m)
            l = jnp.sum(p_d, axis=-1, keepdims=True)
            acc = jnp.dot(p_d.astype(BF16), v_ref[lo:lo + tq, vo], preferred_element_type=F32)
            if i > 0:
                p_o = jnp.exp2(s_o - m)
                l = l + jnp.sum(p_o, axis=-1, keepdims=True)
                acc = acc + jnp.dot(p_o.astype(BF16), v_ref[0:lo, vo], preferred_element_type=F32)
            o_ref[lo:lo + tq, vo] = (acc / l).astype(BF16)


def _attention(q, k, v, bsz, tq, heads):
    t = q.shape[0]
    return pl.pallas_call(
        functools.partial(_attn_kernel, tq=tq, heads=heads),
        grid=(bsz, MLA_HEADS // heads),
        in_specs=[
            pl.BlockSpec((SEQ, heads * HEAD_PAD), lambda b, h: (b, h)),
            pl.BlockSpec((SEQ, heads * HEAD_PAD), lambda b, h: (b, h)),
            pl.BlockSpec((SEQ, heads * V_HEAD), lambda b, h: (b, h)),
        ],
        out_specs=pl.BlockSpec((SEQ, heads * V_HEAD), lambda b, h: (b, h)),
        out_shape=jax.ShapeDtypeStruct((t, MLA_HEADS * V_HEAD), BF16),
        compiler_params=_params(("parallel", "parallel")),
        name="attn",
    )(q, k, v)


def _merge_kernel(o_ref, yag_ref, gb_ref, x_ref, mod_ref, pg_ref, wb_ref, wo_ref, out_ref):
    yb = jnp.dot(o_ref[...], wb_ref[...], preferred_element_type=F32)
    merged = yag_ref[...].astype(F32) + _sigmoid(gb_ref[...].astype(F32)) * yb
    y = jnp.dot(merged.astype(BF16), wo_ref[...], preferred_element_type=F32)
    _residual_rms_to(out_ref, x_ref, y, mod_ref[0, 2:3, :] * pg_ref[...])


def _merge(o, yag, z, x2d, mod, post_g, wb, wo, tm):
    t = x2d.shape[0]
    tiles_per_seq = SEQ // tm
    const = lambda i: (0, 0)
    return pl.pallas_call(
        _merge_kernel,
        grid=(t // tm,),
        in_specs=[
            pl.BlockSpec((tm, D_MODEL), lambda i: (i, 0)),
            pl.BlockSpec((tm, D_MODEL), lambda i: (i, 0)),
            pl.BlockSpec((tm, D_MODEL), lambda i: (i, 3)),
            pl.BlockSpec((tm, D_MODEL), lambda i: (i, 0)),
            pl.BlockSpec((1, N_MOD, D_MODEL), lambda i: (i // tiles_per_seq, 0, 0)),
            pl.BlockSpec((1, D_MODEL), const),
            pl.BlockSpec(wb.shape, const),
            pl.BlockSpec(wo.shape, const),
        ],
        out_specs=pl.BlockSpec((tm, D_MODEL), lambda i: (i, 0)),
        out_shape=jax.ShapeDtypeStruct((t, D_MODEL), F32),
        compiler_params=_params(("parallel",)),
        name="merge",
    )(o, yag, z, x2d, mod, post_g, wb, wo)


def _ffn_up_kernel(x_ref, halo_ref, mod_ref, g_ref, wg_ref, wv_ref, cwg_ref, cwv_ref, cbg_ref, cbv_ref,
                   act_ref, h_ref, *, tiles_per_seq):
    tm = x_ref.shape[0]
    i = pl.program_id(0)

    @pl.when(pl.program_id(1) == 0)
    def _():
        gain = g_ref[...] * (1.0 + mod_ref[0, 4:5, :])
        shift = mod_ref[0, 3:4, :]
        _modulated_rms_to(h_ref, HALO, x_ref, gain, shift)
        keep = (i % tiles_per_seq != 0).astype(F32)
        _modulated_rms_to(h_ref, 0, halo_ref, gain * keep, shift * keep)

    hext = h_ref[...]

    def conv(w_ref, cw_ref, cb_ref):
        up = jnp.dot(hext, w_ref[...].astype(BF16), preferred_element_type=F32)
        y = cw_ref[0:1, :] * pltpu.roll(up, 2, axis=0)[HALO:HALO + tm, :]
        y = y + cw_ref[1:2, :] * pltpu.roll(up, 1, axis=0)[HALO:HALO + tm, :]
        y = y + cw_ref[2:3, :] * up[HALO:HALO + tm, :]
        return y + cb_ref[...]

    gate = conv(wg_ref, cwg_ref, cbg_ref)
    val = conv(wv_ref, cwv_ref, cbv_ref)
    act_ref[...] = (gate * jax.nn.sigmoid(gate) * val).astype(BF16)


def _ffn_up(x1, mod, pre_g, w_up, conv_w, conv_b, tm, tn):
    t = x1.shape[0]
    tiles_per_seq = SEQ // tm
    nj = D_FF // tn
    halo_blocks = tm // HALO
    return pl.pallas_call(
        functools.partial(_ffn_up_kernel, tiles_per_seq=tiles_per_seq),
        grid=(t // tm, nj),
        in_specs=[
            pl.BlockSpec((tm, D_MODEL), lambda i, j: (i, 0)),
            pl.BlockSpec((HALO, D_MODEL), lambda i, j: (jnp.maximum(i * halo_blocks - 1, 0), 0)),
            pl.BlockSpec((1, N_MOD, D_MODEL), lambda i, j: (i // tiles_per_seq, 0, 0)),
            pl.BlockSpec((1, D_MODEL), lambda i, j: (0, 0)),
            pl.BlockSpec((D_MODEL, tn), lambda i, j: (0, j)),
            pl.BlockSpec((D_MODEL, tn), lambda i, j: (0, nj + j)),
            pl.BlockSpec((CONV_W, tn), lambda i, j: (0, j)),
            pl.BlockSpec((CONV_W, tn), lambda i, j: (0, nj + j)),
            pl.BlockSpec((1, tn), lambda i, j: (0, j)),
            pl.BlockSpec((1, tn), lambda i, j: (0, nj + j)),
        ],
        out_specs=pl.BlockSpec((tm, tn), lambda i, j: (i, j)),
        out_shape=jax.ShapeDtypeStruct((t, D_FF), BF16),
        scratch_shapes=[pltpu.VMEM((HALO + tm, D_MODEL), BF16)],
        compiler_params=_params(("parallel", "arbitrary")),
        name="ffn_up",
    )(x1, x1, mod, pre_g, w_up, w_up, conv_w, conv_w, conv_b, conv_b)


def _ffn_down_kernel(act_ref, x_ref, mod_ref, pg_ref, wd_ref, out_ref):
    y = jnp.dot(act_ref[...], wd_ref[...], preferred_element_type=F32)
    _residual_rms_to(out_ref, x_ref, y, mod_ref[0, 5:6, :] * pg_ref[...])


def _ffn_down(act, x1, mod, post_g, w_down, tm):
    t = x1.shape[0]
    tiles_per_seq = SEQ // tm
    return pl.pallas_call(
        _ffn_down_kernel,
        grid=(t // tm,),
        in_specs=[
            pl.BlockSpec((tm, D_FF), lambda i: (i, 0)),
            pl.BlockSpec((tm, D_MODEL), lambda i: (i, 0)),
            pl.BlockSpec((1, N_MOD, D_MODEL), lambda i: (i // tiles_per_seq, 0, 0)),
            pl.BlockSpec((1, D_MODEL), lambda i: (0, 0)),
            pl.BlockSpec((D_FF, D_MODEL), lambda i: (0, 0), pipeline_mode=pl.Buffered(1)),
        ],
        out_specs=pl.BlockSpec((tm, D_MODEL), lambda i: (i, 0)),
        out_shape=jax.ShapeDtypeStruct((t, D_MODEL), F32),
        compiler_params=_params(("parallel",)),
        name="ffn_down",
    )(act, x1, mod, post_g, w_down)


W_IN_Q0, W_IN_PE0, W_IN_GA0 = 4096, 4864, 4928


def _layout_w_in_kernel(w_ref, o_ref):
    j = pl.program_id(0)

    @pl.when(j < N_Z_TILES - 1)
    def _():
        o_ref[...] = w_ref[...].astype(BF16)

    @pl.when(j == N_Z_TILES - 1)
    def _():
        n_lat = W_IN_PE0 - W_IN_Q0
        half = QK_ROPE // 2
        o_ref[0:n_lat + QK_ROPE, :] = w_ref[0:n_lat + QK_ROPE, :].astype(BF16)
        zeros = jnp.zeros((LANES - QK_ROPE, D_MODEL), BF16)
        o_ref[n_lat + QK_ROPE:n_lat + LANES, :] = zeros
        o_ref[n_lat + LANES:n_lat + LANES + half, :] = w_ref[n_lat + half:n_lat + QK_ROPE, :].astype(BF16)
        o_ref[n_lat + LANES + half:n_lat + LANES + QK_ROPE, :] = w_ref[n_lat:n_lat + half, :].astype(BF16)
        o_ref[n_lat + LANES + QK_ROPE:, :] = zeros


def _layout_w_in(w_in_t):
    g = 64
    def src_row(j):
        blk = jnp.where(j < 4, j * (Z_TILE // g),
                        jnp.where(j < 8, W_IN_GA0 // g + (j - 4) * (Z_TILE // g), W_IN_Q0 // g))
        return blk * g
    return pl.pallas_call(
        _layout_w_in_kernel,
        grid=(N_Z_TILES,),
        in_specs=[pl.BlockSpec((pl.Element(Z_TILE), pl.Element(D_MODEL)), lambda j: (src_row(j), 0))],
        out_specs=pl.BlockSpec((Z_TILE, D_MODEL), lambda j: (j, 0)),
        out_shape=jax.ShapeDtypeStruct((Z_COLS, D_MODEL), BF16),
        compiler_params=_params(("parallel",)),
        name="layout_w_in",
    )(w_in_t)


def _layout_w_uq(w_uq):
    w = w_uq.reshape(Q_LORA, MLA_HEADS, QK_NOPE + QK_ROPE)
    half = QK_ROPE // 2
    zeros = jnp.zeros((Q_LORA, MLA_HEADS, LANES - QK_ROPE), w.dtype)
    main = jnp.concatenate([w, zeros], axis=-1).reshape(Q_LORA, MLA_HEADS * HEAD_PAD)
    swap = jnp.concatenate([w[:, :, QK_NOPE + half:], w[:, :, QK_NOPE:QK_NOPE + half], zeros], axis=-1)
    return main.astype(BF16), swap.reshape(Q_LORA, MLA_HEADS * LANES).astype(BF16)


def _layout_w_ukv(w_ukv):
    w = w_ukv.reshape(KV_LORA, MLA_HEADS, QK_NOPE + V_HEAD)
    wk = w[:, :, :QK_NOPE].reshape(KV_LORA, MLA_HEADS * QK_NOPE)
    wv = w[:, :, QK_NOPE:].reshape(KV_LORA, MLA_HEADS * V_HEAD)
    return wk.astype(BF16), wv.astype(BF16)


def kernel(x, c, positions, w_ada, b_ada, pre_norm1_g, w_in, gm_ln_g, gm_ln_b, gm_w_s, gm_b_s, w_branch_a, q_norm_g, w_uq, kv_norm_g, w_ukv, w_branch_b, w_out, post_norm1_g, pre_norm2_g, w_up, conv_w, conv_b, w_down, post_norm2_g):
    bsz, seq, d = x.shape
    t = bsz * seq
    x2d = x.reshape(t, d)
    row = lambda a: a.reshape(1, -1)

    cos, sin = _rope_tables(positions)
    pad = jnp.zeros((t, LANES - QK_ROPE), F32)
    cos_t = jnp.concatenate([cos, cos, pad], axis=1)
    sin_t = jnp.concatenate([-sin, sin, pad], axis=1)

    w_zt = _layout_w_in(w_in.T)
    wq, wqs = _layout_w_uq(w_uq)
    wk, wv = _layout_w_ukv(w_ukv)
    b_s_bcast = jnp.broadcast_to(gm_b_s[:, :, None], (GM_GROUPS, CHUNK, CHUNK))

    mod = _ada(c, w_ada, b_ada).reshape(bsz, N_MOD, d)
    z = _inproj(x2d, mod, row(pre_norm1_g), w_zt, tm=1024)
    yag = _gmlp(z, row(gm_ln_g), row(gm_ln_b), gm_w_s, b_s_bcast, w_branch_a, tm=512)
    q, k, v = _mla_proj(z, cos_t, sin_t, row(q_norm_g), row(kv_norm_g), wq, wqs, wk, wv, tm=512)
    o = _attention(q, k, v, bsz, tq=512, heads=4)
    x1 = _merge(o, yag, z, x2d, mod, row(post_norm1_g), w_branch_b.astype(BF16), w_out.astype(BF16), tm=256)
    act = _ffn_up(x1, mod, row(pre_norm2_g), w_up, conv_w, row(conv_b), tm=1024, tn=512)
    x2 = _ffn_down(act, x1, mod, row(post_norm2_g), w_down.astype(BF16), tm=512)
    return x2.reshape(bsz, seq, d)
```

```python
import functools

import jax
import jax.numpy as jnp
from jax import lax
from jax.experimental import pallas as pl
from jax.experimental.pallas import tpu as pltpu

F32 = jnp.float32
BF16 = jnp.bfloat16

D_MODEL = 2048
SEQ = 2048
GM_WIDTH = 2048
CHUNK = 128
GM_GROUPS = 16
MLA_HEADS = 16
Q_LORA = 512
KV_LORA = 256
QK_NOPE = 128
QK_ROPE = 64
V_HEAD = 128
ROPE_THETA = 10000.0
D_FF = 5632
CONV_W = 3
EPS = 1e-6
N_MOD = 6
LOG2_E = 1.4426950408889634

LANES = 128
HEAD_PAD = 2 * LANES
HALO = 16
VMEM_LIMIT = 56 * 1024 * 1024

Z_TILE = 1024
Z_COLS = 9 * Z_TILE


def _rms(x, g):
    return x * lax.rsqrt(jnp.mean(x * x, axis=-1, keepdims=True) + EPS) * g


def _sigmoid(x):
    return 0.5 + 0.5 * jnp.tanh(0.5 * x)


NORM_ROWS = 128
GMLP_ROWS = 256
MLA_ROWS = 256


def _modulated_rms_to(h_ref, row0, x_ref, gain, shift):
    for r in range(0, x_ref.shape[0], NORM_ROWS):
        n = min(NORM_ROWS, x_ref.shape[0] - r)
        x = x_ref[r:r + n, :]
        inv = lax.rsqrt(jnp.mean(x * x, axis=-1, keepdims=True) + EPS)
        h_ref[row0 + r:row0 + r + n, :] = (x * inv * gain + shift).astype(BF16)


def _residual_rms_to(out_ref, x_ref, y, gain):
    for r in range(0, x_ref.shape[0], NORM_ROWS):
        n = min(NORM_ROWS, x_ref.shape[0] - r)
        yc = y[r:r + n, :]
        inv = lax.rsqrt(jnp.mean(yc * yc, axis=-1, keepdims=True) + EPS)
        out_ref[r:r + n, :] = x_ref[r:r + n, :] + yc * inv * gain


def _params(sem):
    return pltpu.CompilerParams(dimension_semantics=sem, vmem_limit_bytes=VMEM_LIMIT)


def _ada_kernel(c_ref, w_ref, b_ref, o_ref):
    c = c_ref[...]
    sc = (c * jax.nn.sigmoid(c)).astype(BF16)
    o_ref[...] = jnp.dot(sc, w_ref[...].astype(BF16), preferred_element_type=F32) + b_ref[...]


def _ada(c, w_ada, b_ada):
    bsz = c.shape[0]
    n = w_ada.shape[1]
    tn = 1024
    return pl.pallas_call(
        _ada_kernel,
        grid=(n // tn,),
        in_specs=[
            pl.BlockSpec((bsz, D_MODEL), lambda j: (0, 0)),
            pl.BlockSpec((D_MODEL, tn), lambda j: (0, j)),
            pl.BlockSpec((1, tn), lambda j: (0, j)),
        ],
        out_specs=pl.BlockSpec((bsz, tn), lambda j: (0, j)),
        out_shape=jax.ShapeDtypeStruct((bsz, n), F32),
        compiler_params=_params(("arbitrary",)),
        name="ada",
    )(c, w_ada, b_ada.reshape(1, n))


N_Z_TILES = Z_COLS // Z_TILE
INPROJ_TILE = 1536
GELU_COLS = 2 * GM_WIDTH


def _gelu_tanh(x):
    c = 0.7978845608028654
    half = 0.5 * x
    return half + half * jnp.tanh(x * (c + (c * 0.044715) * (x * x)))


def _inproj_kernel(x_ref, mod_ref, g_ref, w_ref, z_ref, h_ref):
    j = pl.program_id(1)

    @pl.when(j == 0)
    def _():
        _modulated_rms_to(h_ref, 0, x_ref, g_ref[...] * (1.0 + mod_ref[0, 1:2, :]), mod_ref[0, 0:1, :])

    acc = lax.dot_general(h_ref[...], w_ref[...], (((1,), (1,)), ((), ())), preferred_element_type=F32)
    n_gelu, mixed_cols = divmod(GELU_COLS, INPROJ_TILE)

    @pl.when(j < n_gelu)
    def _():
        z_ref[...] = _gelu_tanh(acc).astype(BF16)

    @pl.when(j == n_gelu)
    def _():
        z_ref[:, :mixed_cols] = _gelu_tanh(acc[:, :mixed_cols]).astype(BF16)
        z_ref[:, mixed_cols:] = acc[:, mixed_cols:].astype(BF16)

    @pl.when(j > n_gelu)
    def _():
        z_ref[...] = acc.astype(BF16)


def _inproj(x2d, mod, pre_g, w_zt, tm):
    t = x2d.shape[0]
    tiles_per_seq = SEQ // tm
    return pl.pallas_call(
        _inproj_kernel,
        grid=(t // tm, Z_COLS // INPROJ_TILE),
        in_specs=[
            pl.BlockSpec((tm, D_MODEL), lambda i, j: (i, 0)),
            pl.BlockSpec((1, N_MOD, D_MODEL), lambda i, j: (i // tiles_per_seq, 0, 0)),
            pl.BlockSpec((1, D_MODEL), lambda i, j: (0, 0)),
            pl.BlockSpec((INPROJ_TILE, D_MODEL), lambda i, j: (j, 0)),
        ],
        out_specs=pl.BlockSpec((tm, INPROJ_TILE), lambda i, j: (i, j)),
        out_shape=jax.ShapeDtypeStruct((t, Z_COLS), BF16),
        scratch_shapes=[pltpu.VMEM((tm, D_MODEL), BF16)],
        compiler_params=_params(("parallel", "arbitrary")),
        name="inproj",
    )(x2d, mod, pre_g, w_zt)


def _gmlp_kernel(u_ref, v_ref, ga_ref, lng_ref, lnb_ref, ws_ref, bs_ref, wa32_ref, o_ref, vln_ref, a_ref, wa_ref):
    tm = u_ref.shape[0]

    @pl.when(pl.program_id(0) == 0)
    def _():
        wa_ref[...] = wa32_ref[...].astype(BF16)

    row = lax.broadcasted_iota(jnp.int32, (CHUNK, CHUNK), 0)
    col = lax.broadcasted_iota(jnp.int32, (CHUNK, CHUNK), 1)
    mask = (col <= row).astype(F32)
    wsm = [(ws_ref[g] * mask).astype(BF16) for g in range(GM_GROUPS)]
    for r0 in range(0, tm, GMLP_ROWS):
        for c0 in range(r0, r0 + GMLP_ROWS, CHUNK):
            rs = slice(c0, c0 + CHUNK)
            v = v_ref[rs, :].astype(F32)
            mu = jnp.mean(v, axis=-1, keepdims=True)
            var = jnp.mean(jnp.square(v - mu), axis=-1, keepdims=True)
            vln_ref[rs, :] = ((v - mu) * lax.rsqrt(var + EPS) * lng_ref[...] + lnb_ref[...]).astype(BF16)
            for g in range(GM_GROUPS):
                cs = slice(g * CHUNK, (g + 1) * CHUNK)
                mixed = jnp.dot(wsm[g], vln_ref[rs, cs], preferred_element_type=F32) + bs_ref[g]
                a_ref[rs, cs] = (u_ref[rs, cs].astype(F32) * mixed).astype(BF16)
        rg = slice(r0, r0 + GMLP_ROWS)
        ya = jnp.dot(a_ref[rg, :], wa_ref[...], preferred_element_type=F32)
        o_ref[rg, :] = (_sigmoid(ga_ref[rg, :].astype(F32)) * ya).astype(BF16)


def _gmlp(z, ln_g, ln_b, w_s, b_s_bcast, w_a, tm):
    t = z.shape[0]
    return pl.pallas_call(
        _gmlp_kernel,
        grid=(t // tm,),
        in_specs=[
            pl.BlockSpec((tm, GM_WIDTH), lambda i: (i, 0)),
            pl.BlockSpec((tm, GM_WIDTH), lambda i: (i, 1)),
            pl.BlockSpec((tm, D_MODEL), lambda i: (i, 2)),
            pl.BlockSpec((1, GM_WIDTH), lambda i: (0, 0)),
            pl.BlockSpec((1, GM_WIDTH), lambda i: (0, 0)),
            pl.BlockSpec((GM_GROUPS, CHUNK, CHUNK), lambda i: (0, 0, 0)),
            pl.BlockSpec((GM_GROUPS, CHUNK, CHUNK), lambda i: (0, 0, 0)),
            pl.BlockSpec((GM_WIDTH, D_MODEL), lambda i: (0, 0), pipeline_mode=pl.Buffered(1)),
        ],
        out_specs=pl.BlockSpec((tm, D_MODEL), lambda i: (i, 0)),
        out_shape=jax.ShapeDtypeStruct((t, D_MODEL), BF16),
        scratch_shapes=[pltpu.VMEM((tm, GM_WIDTH), BF16), pltpu.VMEM((tm, GM_WIDTH), BF16),
                        pltpu.VMEM((GM_WIDTH, D_MODEL), BF16)],
        compiler_params=_params(("arbitrary",)),
        name="gmlp",
    )(z, z, z, ln_g, ln_b, w_s, b_s_bcast, w_a)


ROPE_HALF = QK_ROPE // 2
ROPE_PER_ROW = LANES // ROPE_HALF


def _rope_kernel(pos_ref, inv_ref, cos_ref, sin_ref):
    rows = pos_ref.shape[0]
    ang = pos_ref[...].astype(F32) * inv_ref[...]
    cos_d = jnp.cos(ang)
    sin_d = jnp.sin(ang)
    lane = lax.broadcasted_iota(jnp.int32, (rows, LANES), 1)
    sign = jnp.where(lane < ROPE_HALF, -1.0, 1.0)
    for k in range(ROPE_PER_ROW):
        def widen(d):
            g = d if k == 0 else pltpu.roll(d, LANES - ROPE_HALF * k, axis=1)
            g = jnp.where(lane < ROPE_HALF, g, pltpu.roll(g, ROPE_HALF, axis=1))
            return jnp.where(lane < QK_ROPE, g, 0.0)
        cos_ref[pl.ds(k, rows, stride=ROPE_PER_ROW), :] = widen(cos_d)
        sin_ref[pl.ds(k, rows, stride=ROPE_PER_ROW), :] = widen(sin_d) * sign


def _rope_tables(positions):
    t = positions.size
    inv = ROPE_THETA ** (-jnp.arange(0, QK_ROPE, 2, dtype=F32) / QK_ROPE)
    rows = t // ROPE_PER_ROW
    pos_d = jnp.repeat(positions.reshape(rows, ROPE_PER_ROW), ROPE_HALF, axis=1)
    tr = 512
    return pl.pallas_call(
        _rope_kernel,
        grid=(rows // tr,),
        in_specs=[pl.BlockSpec((tr, LANES), lambda i: (i, 0)), pl.BlockSpec((1, LANES), lambda i: (0, 0))],
        out_specs=[pl.BlockSpec((tr * ROPE_PER_ROW, LANES), lambda i: (i, 0))] * 2,
        out_shape=[jax.ShapeDtypeStruct((t, LANES), F32)] * 2,
        compiler_params=_params(("parallel",)),
        name="rope_tables",
    )(pos_d, jnp.tile(inv, ROPE_PER_ROW).reshape(1, LANES))


def _mla_proj_kernel(zl_ref, cos_ref, sin_ref, qg_ref, kvg_ref, wq_ref, wqs_ref, wk_ref, wv_ref,
                     q_ref, k_ref, v_ref):
    scale = LOG2_E * (QK_NOPE + QK_ROPE) ** -0.5
    lat = Q_LORA + KV_LORA
    for r0 in range(0, zl_ref.shape[0], MLA_ROWS):
        rs = slice(r0, r0 + MLA_ROWS)
        cos_t = cos_ref[rs, :]
        sin_t = sin_ref[rs, :]
        ql = _rms(zl_ref[rs, 0:Q_LORA].astype(F32), qg_ref[...]).astype(BF16)
        kvl = _rms(zl_ref[rs, Q_LORA:lat].astype(F32), kvg_ref[...]).astype(BF16)
        kpe = (zl_ref[rs, lat:lat + LANES].astype(F32) * cos_t
               + zl_ref[rs, lat + LANES:lat + 2 * LANES].astype(F32) * sin_t).astype(BF16)
        cos_t = cos_t * scale
        sin_t = sin_t * scale
        qm = jnp.dot(ql, wq_ref[...], preferred_element_type=F32)
        qs = jnp.dot(ql, wqs_ref[...], preferred_element_type=F32)
        kn = jnp.dot(kvl, wk_ref[...], preferred_element_type=F32)
        v_ref[rs, :] = jnp.dot(kvl, wv_ref[...], preferred_element_type=F32).astype(BF16)
        for h in range(MLA_HEADS):
            o = h * HEAD_PAD
            q_ref[rs, o:o + LANES] = (qm[:, o:o + LANES] * scale).astype(BF16)
            q_ref[rs, o + LANES:o + HEAD_PAD] = (
                qm[:, o + LANES:o + HEAD_PAD] * cos_t + qs[:, h * LANES:(h + 1) * LANES] * sin_t).astype(BF16)
            k_ref[rs, o:o + LANES] = kn[:, h * LANES:(h + 1) * LANES].astype(BF16)
            k_ref[rs, o + LANES:o + HEAD_PAD] = kpe


def _mla_proj(z, cos_t, sin_t, q_g, kv_g, wq, wqs, wk, wv, tm):
    t = z.shape[0]
    hw = MLA_HEADS * HEAD_PAD
    const = lambda i: (0, 0)
    return pl.pallas_call(
        _mla_proj_kernel,
        grid=(t // tm,),
        in_specs=[
            pl.BlockSpec((tm, Z_TILE), lambda i: (i, 8)),
            pl.BlockSpec((tm, LANES), lambda i: (i, 0)),
            pl.BlockSpec((tm, LANES), lambda i: (i, 0)),
            pl.BlockSpec((1, Q_LORA), const),
            pl.BlockSpec((1, KV_LORA), const),
            pl.BlockSpec(wq.shape, const),
            pl.BlockSpec(wqs.shape, const),
            pl.BlockSpec(wk.shape, const),
            pl.BlockSpec(wv.shape, const),
        ],
        out_specs=[
            pl.BlockSpec((tm, hw), lambda i: (i, 0)),
            pl.BlockSpec((tm, hw), lambda i: (i, 0)),
            pl.BlockSpec((tm, MLA_HEADS * V_HEAD), lambda i: (i, 0)),
        ],
        out_shape=[
            jax.ShapeDtypeStruct((t, hw), BF16),
            jax.ShapeDtypeStruct((t, hw), BF16),
            jax.ShapeDtypeStruct((t, MLA_HEADS * V_HEAD), BF16),
        ],
        compiler_params=_params(("parallel",)),
        name="mla_proj",
    )(z, cos_t, sin_t, q_g, kv_g, wq, wqs, wk, wv)


def _attn_kernel(q_ref, k_ref, v_ref, o_ref, *, tq, heads):
    nt = (((1,), (1,)), ((), ()))
    row = lax.broadcasted_iota(jnp.int32, (tq, tq), 0)
    col = lax.broadcasted_iota(jnp.int32, (tq, tq), 1)
    for h in range(heads):
        for i in reversed(range(SEQ // tq)):
            lo = i * tq
            qk = slice(h * HEAD_PAD, (h + 1) * HEAD_PAD)
            vo = slice(h * V_HEAD, (h + 1) * V_HEAD)
            q = q_ref[lo:lo + tq, qk]
            s_d = lax.dot_general(q, k_ref[lo:lo + tq, qk], nt, preferred_element_type=F32)
            s_d = jnp.where(col <= row, s_d, -1e30)
            m = jnp.max(s_d, axis=-1, keepdims=True)
            if i > 0:
                s_o = lax.dot_general(q, k_ref[0:lo, qk], nt, preferred_element_type=F32)
                m = jnp.maximum(m, jnp.max(s_o, axis=-1, keepdims=True))
            p_d = jnp.exp2(s_d - m)
            l = jnp.sum(p_d, axis=-1, keepdims=True)
            acc = jnp.dot(p_d.astype(BF16), v_ref[lo:lo + tq, vo], preferred_element_type=F32)
            if i > 0:
                p_o = jnp.exp2(s_o - m)
                l = l + jnp.sum(p_o, axis=-1, keepdims=True)
                acc = acc + jnp.dot(p_o.astype(BF16), v_ref[0:lo, vo], preferred_element_type=F32)
            o_ref[lo:lo + tq, vo] = (acc / l).astype(BF16)


def _attention(q, k, v, bsz, tq, heads):
    t = q.shape[0]
    return pl.pallas_call(
        functools.partial(_attn_kernel, tq=tq, heads=heads),
        grid=(bsz, MLA_HEADS // heads),
        in_specs=[
            pl.BlockSpec((SEQ, heads * HEAD_PAD), lambda b, h: (b, h)),
            pl.BlockSpec((SEQ, heads * HEAD_PAD), lambda b, h: (b, h)),
            pl.BlockSpec((SEQ, heads * V_HEAD), lambda b, h: (b, h)),
        ],
        out_specs=pl.BlockSpec((SEQ, heads * V_HEAD), lambda b, h: (b, h)),
        out_shape=jax.ShapeDtypeStruct((t, MLA_HEADS * V_HEAD), BF16),
        compiler_params=_params(("parallel", "parallel")),
        name="attn",
    )(q, k, v)


def _merge_kernel(o_ref, yag_ref, gb_ref, x_ref, mod_ref, pg_ref, wb_ref, wo_ref, out_ref):
    yb = jnp.dot(o_ref[...], wb_ref[...], preferred_element_type=F32)
    merged = yag_ref[...].astype(F32) + _sigmoid(gb_ref[...].astype(F32)) * yb
    y = jnp.dot(merged.astype(BF16), wo_ref[...], preferred_element_type=F32)
    _residual_rms_to(out_ref, x_ref, y, mod_ref[0, 2:3, :] * pg_ref[...])


def _merge(o, yag, z, x2d, mod, post_g, wb, wo, tm):
    t = x2d.shape[0]
    tiles_per_seq = SEQ // tm
    const = lambda i: (0, 0)
    return pl.pallas_call(
        _merge_kernel,
        grid=(t // tm,),
        in_specs=[
            pl.BlockSpec((tm, D_MODEL), lambda i: (i, 0)),
            pl.BlockSpec((tm, D_MODEL), lambda i: (i, 0)),
            pl.BlockSpec((tm, D_MODEL), lambda i: (i, 3)),
            pl.BlockSpec((tm, D_MODEL), lambda i: (i, 0)),
            pl.BlockSpec((1, N_MOD, D_MODEL), lambda i: (i // tiles_per_seq, 0, 0)),
            pl.BlockSpec((1, D_MODEL), const),
            pl.BlockSpec(wb.shape, const),
            pl.BlockSpec(wo.shape, const),
        ],
        out_specs=pl.BlockSpec((tm, D_MODEL), lambda i: (i, 0)),
        out_shape=jax.ShapeDtypeStruct((t, D_MODEL), F32),
        compiler_params=_params(("parallel",)),
        name="merge",
    )(o, yag, z, x2d, mod, post_g, wb, wo)


def _ffn_up_kernel(x_ref, halo_ref, mod_ref, g_ref, wg_ref, wv_ref, cwg_ref, cwv_ref, cbg_ref, cbv_ref,
                   act_ref, h_ref, *, tiles_per_seq):
    tm = x_ref.shape[0]
    i = pl.program_id(0)

    @pl.when(pl.program_id(1) == 0)
    def _():
        gain = g_ref[...] * (1.0 + mod_ref[0, 4:5, :])
        shift = mod_ref[0, 3:4, :]
        _modulated_rms_to(h_ref, HALO, x_ref, gain, shift)
        keep = (i % tiles_per_seq != 0).astype(F32)
        _modulated_rms_to(h_ref, 0, halo_ref, gain * keep, shift * keep)

    hext = h_ref[...]

    def conv(w_ref, cw_ref, cb_ref):
        up = jnp.dot(hext, w_ref[...].astype(BF16), preferred_element_type=F32)
        y = cw_ref[0:1, :] * pltpu.roll(up, 2, axis=0)[HALO:HALO + tm, :]
        y = y + cw_ref[1:2, :] * pltpu.roll(up, 1, axis=0)[HALO:HALO + tm, :]
        y = y + cw_ref[2:3, :] * up[HALO:HALO + tm, :]
        return y + cb_ref[...]

    gate = conv(wg_ref, cwg_ref, cbg_ref)
    val = conv(wv_ref, cwv_ref, cbv_ref)
    act_ref[...] = (gate * jax.nn.sigmoid(gate) * val).astype(BF16)


def _ffn_up(x1, mod, pre_g, w_up, conv_w, conv_b, tm, tn):
    t = x1.shape[0]
    tiles_per_seq = SEQ // tm
    nj = D_FF // tn
    halo_blocks = tm // HALO
    return pl.pallas_call(
        functools.partial(_ffn_up_kernel, tiles_per_seq=tiles_per_seq),
        grid=(t // tm, nj),
        in_specs=[
            pl.BlockSpec((tm, D_MODEL), lambda i, j: (i, 0)),
            pl.BlockSpec((HALO, D_MODEL), lambda i, j: (jnp.maximum(i * halo_blocks - 1, 0), 0)),
            pl.BlockSpec((1, N_MOD, D_MODEL), lambda i, j: (i // tiles_per_seq, 0, 0)),
            pl.BlockSpec((1, D_MODEL), lambda i, j: (0, 0)),
            pl.BlockSpec((D_MODEL, tn), lambda i, j: (0, j)),
            pl.BlockSpec((D_MODEL, tn), lambda i, j: (0, nj + j)),
            pl.BlockSpec((CONV_W, tn), lambda i, j: (0, j)),
            pl.BlockSpec((CONV_W, tn), lambda i, j: (0, nj + j)),
            pl.BlockSpec((1, tn), lambda i, j: (0, j)),
            pl.BlockSpec((1, tn), lambda i, j: (0, nj + j)),
        ],
        out_specs=pl.BlockSpec((tm, tn), lambda i, j: (i, j)),
        out_shape=jax.ShapeDtypeStruct((t, D_FF), BF16),
        scratch_shapes=[pltpu.VMEM((HALO + tm, D_MODEL), BF16)],
        compiler_params=_params(("parallel", "arbitrary")),
        name="ffn_up",
    )(x1, x1, mod, pre_g, w_up, w_up, conv_w, conv_w, conv_b, conv_b)


def _ffn_down_kernel(act_ref, x_ref, mod_ref, pg_ref, wd_ref, out_ref):
    y = jnp.dot(act_ref[...], wd_ref[...], preferred_element_type=F32)
    _residual_rms_to(out_ref, x_ref, y, mod_ref[0, 5:6, :] * pg_ref[...])


def _ffn_down(act, x1, mod, post_g, w_down, tm):
    t = x1.shape[0]
    tiles_per_seq = SEQ // tm
    return pl.pallas_call(
        _ffn_down_kernel,
        grid=(t // tm,),
        in_specs=[
            pl.BlockSpec((tm, D_FF), lambda i: (i, 0)),
            pl.BlockSpec((tm, D_MODEL), lambda i: (i, 0)),
            pl.BlockSpec((1, N_MOD, D_MODEL), lambda i: (i // tiles_per_seq, 0, 0)),
            pl.BlockSpec((1, D_MODEL), lambda i: (0, 0)),
            pl.BlockSpec((D_FF, D_MODEL), lambda i: (0, 0), pipeline_mode=pl.Buffered(1)),
        ],
        out_specs=pl.BlockSpec((tm, D_MODEL), lambda i: (i, 0)),
        out_shape=jax.ShapeDtypeStruct((t, D_MODEL), F32),
        compiler_params=_params(("parallel",)),
        name="ffn_down",
    )(act, x1, mod, post_g, w_down)


W_IN_Q0, W_IN_PE0, W_IN_GA0 = 4096, 4864, 4928


def _layout_w_in_kernel(w_ref, o_ref):
    j = pl.program_id(0)

    @pl.when(j < N_Z_TILES - 1)
    def _():
        o_ref[...] = w_ref[...].astype(BF16)

    @pl.when(j == N_Z_TILES - 1)
    def _():
        n_lat = W_IN_PE0 - W_IN_Q0
        half = QK_ROPE // 2
        o_ref[0:n_lat + QK_ROPE, :] = w_ref[0:n_lat + QK_ROPE, :].astype(BF16)
        zeros = jnp.zeros((LANES - QK_ROPE, D_MODEL), BF16)
        o_ref[n_lat + QK_ROPE:n_lat + LANES, :] = zeros
        o_ref[n_lat + LANES:n_lat + LANES + half, :] = w_ref[n_lat + half:n_lat + QK_ROPE, :].astype(BF16)
        o_ref[n_lat + LANES + half:n_lat + LANES + QK_ROPE, :] = w_ref[n_lat:n_lat + half, :].astype(BF16)
        o_ref[n_lat + LANES + QK_ROPE:, :] = zeros


def _layout_w_in(w_in_t):
    g = 64
    def src_row(j):
        blk = jnp.where(j < 4, j * (Z_TILE // g),
                        jnp.where(j < 8, W_IN_GA0 // g + (j - 4) * (Z_TILE // g), W_IN_Q0 // g))
        return blk * g
    return pl.pallas_call(
        _layout_w_in_kernel,
        grid=(N_Z_TILES,),
        in_specs=[pl.BlockSpec((pl.Element(Z_TILE), pl.Element(D_MODEL)), lambda j: (src_row(j), 0))],
        out_specs=pl.BlockSpec((Z_TILE, D_MODEL), lambda j: (j, 0)),
        out_shape=jax.ShapeDtypeStruct((Z_COLS, D_MODEL), BF16),
        compiler_params=_params(("parallel",)),
        name="layout_w_in",
    )(w_in_t)


def _layout_mla_kernel(wuq_ref, wukv_ref, wq_ref, wqs_ref, wk_ref, wv_ref):
    half = QK_ROPE // 2
    zeros = jnp.zeros((Q_LORA, LANES - QK_ROPE), F32)
    for h in range(MLA_HEADS):
        s = h * (QK_NOPE + QK_ROPE)
        x1 = wuq_ref[:, s + QK_NOPE:s + QK_NOPE + half]
        x2 = wuq_ref[:, s + QK_NOPE + half:s + QK_NOPE + QK_ROPE]
        wq_ref[:, h * HEAD_PAD:h * HEAD_PAD + LANES] = wuq_ref[:, s:s + QK_NOPE].astype(BF16)
        wq_ref[:, h * HEAD_PAD + LANES:(h + 1) * HEAD_PAD] = jnp.concatenate([x1, x2, zeros], axis=1).astype(BF16)
        wqs_ref[:, h * LANES:(h + 1) * LANES] = jnp.concatenate([x2, x1, zeros], axis=1).astype(BF16)
        t = h * (QK_NOPE + V_HEAD)
        wk_ref[:, h * QK_NOPE:(h + 1) * QK_NOPE] = wukv_ref[:, t:t + QK_NOPE].astype(BF16)
        wv_ref[:, h * V_HEAD:(h + 1) * V_HEAD] = wukv_ref[:, t + QK_NOPE:t + QK_NOPE + V_HEAD].astype(BF16)


def _layout_mla(w_uq, w_ukv):
    full = lambda a: pl.BlockSpec(a, lambda i: (0, 0))
    shapes = [(Q_LORA, MLA_HEADS * HEAD_PAD), (Q_LORA, MLA_HEADS * LANES),
              (KV_LORA, MLA_HEADS * QK_NOPE), (KV_LORA, MLA_HEADS * V_HEAD)]
    return pl.pallas_call(
        _layout_mla_kernel,
        grid=(1,),
        in_specs=[full(w_uq.shape), full(w_ukv.shape)],
        out_specs=[full(s) for s in shapes],
        out_shape=[jax.ShapeDtypeStruct(s, BF16) for s in shapes],
        compiler_params=_params(("arbitrary",)),
        name="layout_mla",
    )(w_uq, w_ukv)


def kernel(x, c, positions, w_ada, b_ada, pre_norm1_g, w_in, gm_ln_g, gm_ln_b, gm_w_s, gm_b_s, w_branch_a, q_norm_g, w_uq, kv_norm_g, w_ukv, w_branch_b, w_out, post_norm1_g, pre_norm2_g, w_up, conv_w, conv_b, w_down, post_norm2_g):
    bsz, seq, d = x.shape
    t = bsz * seq
    x2d = x.reshape(t, d)
    row = lambda a: a.reshape(1, -1)

    cos_t, sin_t = _rope_tables(positions)

    w_zt = _layout_w_in(w_in.T)
    wq, wqs, wk, wv = _layout_mla(w_uq, w_ukv)
    b_s_bcast = jnp.broadcast_to(gm_b_s[:, :, None], (GM_GROUPS, CHUNK, CHUNK))

    mod = _ada(c, w_ada, b_ada).reshape(bsz, N_MOD, d)
    z = _inproj(x2d, mod, row(pre_norm1_g), w_zt, tm=1024)
    yag = _gmlp(z, row(gm_ln_g), row(gm_ln_b), gm_w_s, b_s_bcast, w_branch_a, tm=512)
    q, k, v = _mla_proj(z, cos_t, sin_t, row(q_norm_g), row(kv_norm_g), wq, wqs, wk, wv, tm=512)
    o = _attention(q, k, v, bsz, tq=512, heads=4)
    x1 = _merge(o, yag, z, x2d, mod, row(post_norm1_g), w_branch_b.astype(BF16), w_out.astype(BF16), tm=256)
    act = _ffn_up(x1, mod, row(pre_norm2_g), w_up, conv_w, row(conv_b), tm=1024, tn=512)
    x2 = _ffn_down(act, x1, mod, row(post_norm2_g), w_down.astype(BF16), tm=512)
    return x2.reshape(bsz, seq, d)
```

```python
import functools

import jax
import jax.numpy as jnp
from jax import lax
from jax.experimental import pallas as pl
from jax.experimental.pallas import tpu as pltpu

F32 = jnp.float32
BF16 = jnp.bfloat16

D_MODEL = 2048
SEQ = 2048
GM_WIDTH = 2048
CHUNK = 128
GM_GROUPS = 16
MLA_HEADS = 16
Q_LORA = 512
KV_LORA = 256
QK_NOPE = 128
QK_ROPE = 64
V_HEAD = 128
ROPE_THETA = 10000.0
D_FF = 5632
CONV_W = 3
EPS = 1e-6
N_MOD = 6
LOG2_E = 1.4426950408889634

LANES = 128
HEAD_PAD = 2 * LANES
HALO = 16
VMEM_LIMIT = 56 * 1024 * 1024

Z_TILE = 1024
Z_COLS = 9 * Z_TILE


def _rms(x, g):
    return x * lax.rsqrt(jnp.mean(x * x, axis=-1, keepdims=True) + EPS) * g


def _sigmoid(x):
    return 0.5 + 0.5 * jnp.tanh(0.5 * x)


NORM_ROWS = 128
GMLP_ROWS = 256
MLA_ROWS = 256


def _modulated_rms_to(h_ref, row0, x_ref, gain, shift):
    for r in range(0, x_ref.shape[0], NORM_ROWS):
        n = min(NORM_ROWS, x_ref.shape[0] - r)
        x = x_ref[r:r + n, :]
        inv = lax.rsqrt(jnp.mean(x * x, axis=-1, keepdims=True) + EPS)
        h_ref[row0 + r:row0 + r + n, :] = (x * inv * gain + shift).astype(BF16)


def _residual_rms_to(out_ref, x_ref, y, gain):
    for r in range(0, x_ref.shape[0], NORM_ROWS):
        n = min(NORM_ROWS, x_ref.shape[0] - r)
        yc = y[r:r + n, :]
        inv = lax.rsqrt(jnp.mean(yc * yc, axis=-1, keepdims=True) + EPS)
        out_ref[r:r + n, :] = x_ref[r:r + n, :] + yc * inv * gain


def _params(sem):
    return pltpu.CompilerParams(dimension_semantics=sem, vmem_limit_bytes=VMEM_LIMIT)


def _ada_kernel(c_ref, w_ref, b_ref, o_ref):
    c = c_ref[...]
    sc = (c * jax.nn.sigmoid(c)).astype(BF16)
    o_ref[...] = jnp.dot(sc, w_ref[...].astype(BF16), preferred_element_type=F32) + b_ref[...]


def _ada(c, w_ada, b_ada):
    bsz = c.shape[0]
    n = w_ada.shape[1]
    tn = 1024
    return pl.pallas_call(
        _ada_kernel,
        grid=(n // tn,),
        in_specs=[
            pl.BlockSpec((bsz, D_MODEL), lambda j: (0, 0)),
            pl.BlockSpec((D_MODEL, tn), lambda j: (0, j)),
            pl.BlockSpec((1, tn), lambda j: (0, j)),
        ],
        out_specs=pl.BlockSpec((bsz, tn), lambda j: (0, j)),
        out_shape=jax.ShapeDtypeStruct((bsz, n), F32),
        compiler_params=_params(("arbitrary",)),
        name="ada",
    )(c, w_ada, b_ada.reshape(1, n))


N_Z_TILES = Z_COLS // Z_TILE
INPROJ_TILE = 1536
GELU_COLS = 2 * GM_WIDTH


def _gelu_tanh(x):
    c = 0.7978845608028654
    half = 0.5 * x
    return half + half * jnp.tanh(x * (c + (c * 0.044715) * (x * x)))


def _inproj_kernel(x_ref, mod_ref, g_ref, w_ref, z_ref, h_ref):
    j = pl.program_id(1)

    @pl.when(j == 0)
    def _():
        _modulated_rms_to(h_ref, 0, x_ref, g_ref[...] * (1.0 + mod_ref[0, 1:2, :]), mod_ref[0, 0:1, :])

    acc = lax.dot_general(h_ref[...], w_ref[...], (((1,), (1,)), ((), ())), preferred_element_type=F32)
    n_gelu, mixed_cols = divmod(GELU_COLS, INPROJ_TILE)

    @pl.when(j < n_gelu)
    def _():
        z_ref[...] = _gelu_tanh(acc).astype(BF16)

    @pl.when(j == n_gelu)
    def _():
        z_ref[:, :mixed_cols] = _gelu_tanh(acc[:, :mixed_cols]).astype(BF16)
        z_ref[:, mixed_cols:] = acc[:, mixed_cols:].astype(BF16)

    @pl.when(j > n_gelu)
    def _():
        z_ref[...] = acc.astype(BF16)


def _inproj(x2d, mod, pre_g, w_zt, tm):
    t = x2d.shape[0]
    tiles_per_seq = SEQ // tm
    return pl.pallas_call(
        _inproj_kernel,
        grid=(t // tm, Z_COLS // INPROJ_TILE),
        in_specs=[
            pl.BlockSpec((tm, D_MODEL), lambda i, j: (i, 0)),
            pl.BlockSpec((1, N_MOD, D_MODEL), lambda i, j: (i // tiles_per_seq, 0, 0)),
            pl.BlockSpec((1, D_MODEL), lambda i, j: (0, 0)),
            pl.BlockSpec((INPROJ_TILE, D_MODEL), lambda i, j: (j, 0)),
        ],
        out_specs=pl.BlockSpec((tm, INPROJ_TILE), lambda i, j: (i, j)),
        out_shape=jax.ShapeDtypeStruct((t, Z_COLS), BF16),
        scratch_shapes=[pltpu.VMEM((tm, D_MODEL), BF16)],
        compiler_params=_params(("parallel", "arbitrary")),
        name="inproj",
    )(x2d, mod, pre_g, w_zt)


def _gmlp_kernel(u_ref, v_ref, ga_ref, lng_ref, lnb_ref, ws_ref, bs_ref, wa32_ref, o_ref, vln_ref, a_ref, wa_ref):
    tm = u_ref.shape[0]

    @pl.when(pl.program_id(0) == 0)
    def _():
        wa_ref[...] = wa32_ref[...].astype(BF16)

    row = lax.broadcasted_iota(jnp.int32, (CHUNK, CHUNK), 0)
    col = lax.broadcasted_iota(jnp.int32, (CHUNK, CHUNK), 1)
    mask = (col <= row).astype(F32)
    wsm = [(ws_ref[g] * mask).astype(BF16) for g in range(GM_GROUPS)]
    for r0 in range(0, tm, GMLP_ROWS):
        for c0 in range(r0, r0 + GMLP_ROWS, CHUNK):
            rs = slice(c0, c0 + CHUNK)
            v = v_ref[rs, :].astype(F32)
            mu = jnp.mean(v, axis=-1, keepdims=True)
            var = jnp.mean(jnp.square(v - mu), axis=-1, keepdims=True)
            vln_ref[rs, :] = ((v - mu) * lax.rsqrt(var + EPS) * lng_ref[...] + lnb_ref[...]).astype(BF16)
            for g in range(GM_GROUPS):
                cs = slice(g * CHUNK, (g + 1) * CHUNK)
                mixed = jnp.dot(wsm[g], vln_ref[rs, cs], preferred_element_type=F32) + bs_ref[g]
                a_ref[rs, cs] = (u_ref[rs, cs].astype(F32) * mixed).astype(BF16)
        rg = slice(r0, r0 + GMLP_ROWS)
        ya = jnp.dot(a_ref[rg, :], wa_ref[...], preferred_element_type=F32)
        o_ref[rg, :] = (_sigmoid(ga_ref[rg, :].astype(F32)) * ya).astype(BF16)


def _gmlp(z, ln_g, ln_b, w_s, b_s_bcast, w_a, tm):
    t = z.shape[0]
    return pl.pallas_call(
        _gmlp_kernel,
        grid=(t // tm,),
        in_specs=[
            pl.BlockSpec((tm, GM_WIDTH), lambda i: (i, 0)),
            pl.BlockSpec((tm, GM_WIDTH), lambda i: (i, 1)),
            pl.BlockSpec((tm, D_MODEL), lambda i: (i, 2)),
            pl.BlockSpec((1, GM_WIDTH), lambda i: (0, 0)),
            pl.BlockSpec((1, GM_WIDTH), lambda i: (0, 0)),
            pl.BlockSpec((GM_GROUPS, CHUNK, CHUNK), lambda i: (0, 0, 0)),
            pl.BlockSpec((GM_GROUPS, CHUNK, CHUNK), lambda i: (0, 0, 0)),
            pl.BlockSpec((GM_WIDTH, D_MODEL), lambda i: (0, 0), pipeline_mode=pl.Buffered(1)),
        ],
        out_specs=pl.BlockSpec((tm, D_MODEL), lambda i: (i, 0)),
        out_shape=jax.ShapeDtypeStruct((t, D_MODEL), BF16),
        scratch_shapes=[pltpu.VMEM((tm, GM_WIDTH), BF16), pltpu.VMEM((tm, GM_WIDTH), BF16),
                        pltpu.VMEM((GM_WIDTH, D_MODEL), BF16)],
        compiler_params=_params(("arbitrary",)),
        name="gmlp",
    )(z, z, z, ln_g, ln_b, w_s, b_s_bcast, w_a)


ROPE_HALF = QK_ROPE // 2
ROPE_PER_ROW = LANES // ROPE_HALF


def _rope_kernel(pos_ref, inv_ref, cos_ref, sin_ref):
    rows = pos_ref.shape[0]
    ang = pos_ref[...].astype(F32) * inv_ref[...]
    cos_d = jnp.cos(ang)
    sin_d = jnp.sin(ang)
    lane = lax.broadcasted_iota(jnp.int32, (rows, LANES), 1)
    sign = jnp.where(lane < ROPE_HALF, -1.0, 1.0)
    for k in range(ROPE_PER_ROW):
        def widen(d):
            g = d if k == 0 else pltpu.roll(d, LANES - ROPE_HALF * k, axis=1)
            g = jnp.where(lane < ROPE_HALF, g, pltpu.roll(g, ROPE_HALF, axis=1))
            return jnp.where(lane < QK_ROPE, g, 0.0)
        cos_ref[pl.ds(k, rows, stride=ROPE_PER_ROW), :] = widen(cos_d)
        sin_ref[pl.ds(k, rows, stride=ROPE_PER_ROW), :] = widen(sin_d) * sign


def _rope_tables(positions):
    t = positions.size
    inv = ROPE_THETA ** (-jnp.arange(0, QK_ROPE, 2, dtype=F32) / QK_ROPE)
    rows = t // ROPE_PER_ROW
    pos_d = jnp.repeat(positions.reshape(rows, ROPE_PER_ROW), ROPE_HALF, axis=1)
    tr = 512
    return pl.pallas_call(
        _rope_kernel,
        grid=(rows // tr,),
        in_specs=[pl.BlockSpec((tr, LANES), lambda i: (i, 0)), pl.BlockSpec((1, LANES), lambda i: (0, 0))],
        out_specs=[pl.BlockSpec((tr * ROPE_PER_ROW, LANES), lambda i: (i, 0))] * 2,
        out_shape=[jax.ShapeDtypeStruct((t, LANES), F32)] * 2,
        compiler_params=_params(("parallel",)),
        name="rope_tables",
    )(pos_d, jnp.tile(inv, ROPE_PER_ROW).reshape(1, LANES))


def _mla_proj_kernel(zl_ref, cos_ref, sin_ref, qg_ref, kvg_ref, wq_ref, wqs_ref, wk_ref, wv_ref,
                     q_ref, k_ref, v_ref):
    scale = LOG2_E * (QK_NOPE + QK_ROPE) ** -0.5
    lat = Q_LORA + KV_LORA
    for r0 in range(0, zl_ref.shape[0], MLA_ROWS):
        rs = slice(r0, r0 + MLA_ROWS)
        cos_t = cos_ref[rs, :]
        sin_t = sin_ref[rs, :]
        ql = _rms(zl_ref[rs, 0:Q_LORA].astype(F32), qg_ref[...]).astype(BF16)
        kvl = _rms(zl_ref[rs, Q_LORA:lat].astype(F32), kvg_ref[...]).astype(BF16)
        kpe = (zl_ref[rs, lat:lat + LANES].astype(F32) * cos_t
               + zl_ref[rs, lat + LANES:lat + 2 * LANES].astype(F32) * sin_t).astype(BF16)
        cos_t = cos_t * scale
        sin_t = sin_t * scale
        qm = jnp.dot(ql, wq_ref[...], preferred_element_type=F32)
        qs = jnp.dot(ql, wqs_ref[...], preferred_element_type=F32)
        kn = jnp.dot(kvl, wk_ref[...], preferred_element_type=F32)
        v_ref[rs, :] = jnp.dot(kvl, wv_ref[...], preferred_element_type=F32).astype(BF16)
        for h in range(MLA_HEADS):
            o = h * HEAD_PAD
            q_ref[rs, o:o + LANES] = (qm[:, o:o + LANES] * scale).astype(BF16)
            q_ref[rs, o + LANES:o + HEAD_PAD] = (
                qm[:, o + LANES:o + HEAD_PAD] * cos_t + qs[:, h * LANES:(h + 1) * LANES] * sin_t).astype(BF16)
            k_ref[rs, o:o + LANES] = kn[:, h * LANES:(h + 1) * LANES].astype(BF16)
            k_ref[rs, o + LANES:o + HEAD_PAD] = kpe


def _mla_proj(z, cos_t, sin_t, q_g, kv_g, wq, wqs, wk, wv, tm):
    t = z.shape[0]
    hw = MLA_HEADS * HEAD_PAD
    const = lambda i: (0, 0)
    return pl.pallas_call(
        _mla_proj_kernel,
        grid=(t // tm,),
        in_specs=[
            pl.BlockSpec((tm, Z_TILE), lambda i: (i, 8)),
            pl.BlockSpec((tm, LANES), lambda i: (i, 0)),
            pl.BlockSpec((tm, LANES), lambda i: (i, 0)),
            pl.BlockSpec((1, Q_LORA), const),
            pl.BlockSpec((1, KV_LORA), const),
            pl.BlockSpec(wq.shape, const),
            pl.BlockSpec(wqs.shape, const),
            pl.BlockSpec(wk.shape, const),
            pl.BlockSpec(wv.shape, const),
        ],
        out_specs=[
            pl.BlockSpec((tm, hw), lambda i: (i, 0)),
            pl.BlockSpec((tm, hw), lambda i: (i, 0)),
            pl.BlockSpec((tm, MLA_HEADS * V_HEAD), lambda i: (i, 0)),
        ],
        out_shape=[
            jax.ShapeDtypeStruct((t, hw), BF16),
            jax.ShapeDtypeStruct((t, hw), BF16),
            jax.ShapeDtypeStruct((t, MLA_HEADS * V_HEAD), BF16),
        ],
        compiler_params=_params(("parallel",)),
        name="mla_proj",
    )(z, cos_t, sin_t, q_g, kv_g, wq, wqs, wk, wv)


def _attn_kernel(q_ref, k_ref, v_ref, o_ref, *, tq, heads):
    nt = (((1,), (1,)), ((), ()))
    row = lax.broadcasted_iota(jnp.int32, (tq, tq), 0)
    col = lax.broadcasted_iota(jnp.int32, (tq, tq), 1)
    for h in range(heads):
        for i in reversed(range(SEQ // tq)):
            lo = i * tq
            qk = slice(h * HEAD_PAD, (h + 1) * HEAD_PAD)
            vo = slice(h * V_HEAD, (h + 1) * V_HEAD)
            q = q_ref[lo:lo + tq, qk]
            s_d = lax.dot_general(q, k_ref[lo:lo + tq, qk], nt, preferred_element_type=F32)
            s_d = jnp.where(col <= row, s_d, -1e30)
            m = jnp.max(s_d, axis=-1, keepdims=True)
            if i > 0:
                s_o = lax.dot_general(q, k_ref[0:lo, qk], nt, preferred_element_type=F32)
                m = jnp.maximum(m, jnp.max(s_o, axis=-1, keepdims=True))
            p_d = jnp.exp2(s_d - m)
            l = jnp.sum(p_d, axis=-1, keepdims=True)
            acc = jnp.dot(p_d.astype(BF16), v_ref[lo:lo + tq, vo], preferred_element_type=F32)
            if i > 0:
                p_o = jnp.exp2(s_o - m)
                l = l + jnp.sum(p_o, axis=-1, keepdims=True)
                acc = acc + jnp.dot(p_o.astype(BF16), v_ref[0:lo, vo], preferred_element_type=F32)
            o_ref[lo:lo + tq, vo] = (acc / l).astype(BF16)


def _attention(q, k, v, bsz, tq, heads):
    t = q.shape[0]
    return pl.pallas_call(
        functools.partial(_attn_kernel, tq=tq, heads=heads),
        grid=(bsz, MLA_HEADS // heads),
        in_specs=[
            pl.BlockSpec((SEQ, heads * HEAD_PAD), lambda b, h: (b, h)),
            pl.BlockSpec((SEQ, heads * HEAD_PAD), lambda b, h: (b, h)),
            pl.BlockSpec((SEQ, heads * V_HEAD), lambda b, h: (b, h)),
        ],
        out_specs=pl.BlockSpec((SEQ, heads * V_HEAD), lambda b, h: (b, h)),
        out_shape=jax.ShapeDtypeStruct((t, MLA_HEADS * V_HEAD), BF16),
        compiler_params=_params(("parallel", "parallel")),
        name="attn",
    )(q, k, v)


def _merge_kernel(o_hbm, yag_hbm, z_hbm, x_hbm, mod_hbm, pg_ref, wb_ref, wo_ref, out_hbm, *, tm, tiles_per_seq):
    def body(o_ref, yag_ref, gb_ref, x_ref, mod_ref, out_ref):
        yb = jnp.dot(o_ref[...], wb_ref[...], preferred_element_type=F32)
        merged = yag_ref[...].astype(F32) + _sigmoid(gb_ref[...].astype(F32)) * yb
        y = jnp.dot(merged.astype(BF16), wo_ref[...], preferred_element_type=F32)
        _residual_rms_to(out_ref, x_ref, y, mod_ref[0, 2:3, :] * pg_ref[...])

    rows = pl.BlockSpec((tm, D_MODEL), lambda i: (i, 0))
    pltpu.emit_pipeline(
        body,
        grid=(x_hbm.shape[0] // tm,),
        in_specs=[rows, rows, pl.BlockSpec((tm, D_MODEL), lambda i: (i, 3)), rows,
                  pl.BlockSpec((1, N_MOD, D_MODEL), lambda i: (i // tiles_per_seq, 0, 0))],
        out_specs=[rows],
    )(o_hbm, yag_hbm, z_hbm, x_hbm, mod_hbm, out_hbm)


def _merge(o, yag, z, x2d, mod, post_g, wb, wo, tm):
    t = x2d.shape[0]
    hbm = pl.BlockSpec(memory_space=pl.ANY)
    vmem = pl.BlockSpec(memory_space=pltpu.VMEM)
    return pl.pallas_call(
        functools.partial(_merge_kernel, tm=tm, tiles_per_seq=SEQ // tm),
        in_specs=[hbm, hbm, hbm, hbm, hbm, vmem, vmem, vmem],
        out_specs=hbm,
        out_shape=jax.ShapeDtypeStruct((t, D_MODEL), F32),
        compiler_params=pltpu.CompilerParams(vmem_limit_bytes=VMEM_LIMIT),
        name="merge",
    )(o, yag, z, x2d, mod, post_g, wb, wo)


def _ffn_up_kernel(x_ref, halo_ref, mod_ref, g_ref, wg_ref, wv_ref, cwg_ref, cwv_ref, cbg_ref, cbv_ref,
                   act_ref, h_ref, *, tiles_per_seq):
    tm = x_ref.shape[0]
    i = pl.program_id(0)

    @pl.when(pl.program_id(1) == 0)
    def _():
        gain = g_ref[...] * (1.0 + mod_ref[0, 4:5, :])
        shift = mod_ref[0, 3:4, :]
        _modulated_rms_to(h_ref, HALO, x_ref, gain, shift)
        keep = (i % tiles_per_seq != 0).astype(F32)
        _modulated_rms_to(h_ref, 0, halo_ref, gain * keep, shift * keep)

    hext = h_ref[...]

    def conv(w_ref, cw_ref, cb_ref):
        up = jnp.dot(hext, w_ref[...].astype(BF16), preferred_element_type=F32)
        y = cw_ref[0:1, :] * pltpu.roll(up, 2, axis=0)[HALO:HALO + tm, :]
        y = y + cw_ref[1:2, :] * pltpu.roll(up, 1, axis=0)[HALO:HALO + tm, :]
        y = y + cw_ref[2:3, :] * up[HALO:HALO + tm, :]
        return y + cb_ref[...]

    gate = conv(wg_ref, cwg_ref, cbg_ref)
    val = conv(wv_ref, cwv_ref, cbv_ref)
    act_ref[...] = (gate * jax.nn.sigmoid(gate) * val).astype(BF16)


def _ffn_up(x1, mod, pre_g, w_up, conv_w, conv_b, tm, tn):
    t = x1.shape[0]
    tiles_per_seq = SEQ // tm
    nj = D_FF // tn
    halo_blocks = tm // HALO
    return pl.pallas_call(
        functools.partial(_ffn_up_kernel, tiles_per_seq=tiles_per_seq),
        grid=(t // tm, nj),
        in_specs=[
            pl.BlockSpec((tm, D_MODEL), lambda i, j: (i, 0)),
            pl.BlockSpec((HALO, D_MODEL), lambda i, j: (jnp.maximum(i * halo_blocks - 1, 0), 0)),
            pl.BlockSpec((1, N_MOD, D_MODEL), lambda i, j: (i // tiles_per_seq, 0, 0)),
            pl.BlockSpec((1, D_MODEL), lambda i, j: (0, 0)),
            pl.BlockSpec((D_MODEL, tn), lambda i, j: (0, j)),
            pl.BlockSpec((D_MODEL, tn), lambda i, j: (0, nj + j)),
            pl.BlockSpec((CONV_W, tn), lambda i, j: (0, j)),
            pl.BlockSpec((CONV_W, tn), lambda i, j: (0, nj + j)),
            pl.BlockSpec((1, tn), lambda i, j: (0, j)),
            pl.BlockSpec((1, tn), lambda i, j: (0, nj + j)),
        ],
        out_specs=pl.BlockSpec((tm, tn), lambda i, j: (i, j)),
        out_shape=jax.ShapeDtypeStruct((t, D_FF), BF16),
        scratch_shapes=[pltpu.VMEM((HALO + tm, D_MODEL), BF16)],
        compiler_params=_params(("parallel", "arbitrary")),
        name="ffn_up",
    )(x1, x1, mod, pre_g, w_up, w_up, conv_w, conv_w, conv_b, conv_b)


def _ffn_down_kernel(act_hbm, x_hbm, mod_hbm, pg_ref, wd_ref, out_hbm, *, tm, tiles_per_seq):
    def body(act_ref, x_ref, mod_ref, out_ref):
        y = jnp.dot(act_ref[...], wd_ref[...], preferred_element_type=F32)
        _residual_rms_to(out_ref, x_ref, y, mod_ref[0, 5:6, :] * pg_ref[...])

    pltpu.emit_pipeline(
        body,
        grid=(x_hbm.shape[0] // tm,),
        in_specs=[
            pl.BlockSpec((tm, D_FF), lambda i: (i, 0)),
            pl.BlockSpec((tm, D_MODEL), lambda i: (i, 0)),
            pl.BlockSpec((1, N_MOD, D_MODEL), lambda i: (i // tiles_per_seq, 0, 0)),
        ],
        out_specs=[pl.BlockSpec((tm, D_MODEL), lambda i: (i, 0))],
    )(act_hbm, x_hbm, mod_hbm, out_hbm)


def _ffn_down(act, x1, mod, post_g, w_down, tm):
    t = x1.shape[0]
    hbm = pl.BlockSpec(memory_space=pl.ANY)
    vmem = pl.BlockSpec(memory_space=pltpu.VMEM)
    return pl.pallas_call(
        functools.partial(_ffn_down_kernel, tm=tm, tiles_per_seq=SEQ // tm),
        in_specs=[hbm, hbm, hbm, vmem, vmem],
        out_specs=hbm,
        out_shape=jax.ShapeDtypeStruct((t, D_MODEL), F32),
        compiler_params=pltpu.CompilerParams(vmem_limit_bytes=VMEM_LIMIT),
        name="ffn_down",
    )(act, x1, mod, post_g, w_down)


W_IN_Q0, W_IN_PE0, W_IN_GA0 = 4096, 4864, 4928


def _layout_w_in_kernel(w_ref, o_ref):
    j = pl.program_id(0)

    @pl.when(j < N_Z_TILES - 1)
    def _():
        o_ref[...] = w_ref[...].astype(BF16)

    @pl.when(j == N_Z_TILES - 1)
    def _():
        n_lat = W_IN_PE0 - W_IN_Q0
        half = QK_ROPE // 2
        o_ref[0:n_lat + QK_ROPE, :] = w_ref[0:n_lat + QK_ROPE, :].astype(BF16)
        zeros = jnp.zeros((LANES - QK_ROPE, D_MODEL), BF16)
        o_ref[n_lat + QK_ROPE:n_lat + LANES, :] = zeros
        o_ref[n_lat + LANES:n_lat + LANES + half, :] = w_ref[n_lat + half:n_lat + QK_ROPE, :].astype(BF16)
        o_ref[n_lat + LANES + half:n_lat + LANES + QK_ROPE, :] = w_ref[n_lat:n_lat + half, :].astype(BF16)
        o_ref[n_lat + LANES + QK_ROPE:, :] = zeros


def _layout_w_in(w_in_t):
    g = 64
    def src_row(j):
        blk = jnp.where(j < 4, j * (Z_TILE // g),
                        jnp.where(j < 8, W_IN_GA0 // g + (j - 4) * (Z_TILE // g), W_IN_Q0 // g))
        return blk * g
    return pl.pallas_call(
        _layout_w_in_kernel,
        grid=(N_Z_TILES,),
        in_specs=[pl.BlockSpec((pl.Element(Z_TILE), pl.Element(D_MODEL)), lambda j: (src_row(j), 0))],
        out_specs=pl.BlockSpec((Z_TILE, D_MODEL), lambda j: (j, 0)),
        out_shape=jax.ShapeDtypeStruct((Z_COLS, D_MODEL), BF16),
        compiler_params=_params(("parallel",)),
        name="layout_w_in",
    )(w_in_t)


def _layout_mla_kernel(wuq_ref, wukv_ref, wq_ref, wqs_ref, wk_ref, wv_ref):
    half = QK_ROPE // 2
    zeros = jnp.zeros((Q_LORA, LANES - QK_ROPE), F32)
    for h in range(MLA_HEADS):
        s = h * (QK_NOPE + QK_ROPE)
        x1 = wuq_ref[:, s + QK_NOPE:s + QK_NOPE + half]
        x2 = wuq_ref[:, s + QK_NOPE + half:s + QK_NOPE + QK_ROPE]
        wq_ref[:, h * HEAD_PAD:h * HEAD_PAD + LANES] = wuq_ref[:, s:s + QK_NOPE].astype(BF16)
        wq_ref[:, h * HEAD_PAD + LANES:(h + 1) * HEAD_PAD] = jnp.concatenate([x1, x2, zeros], axis=1).astype(BF16)
        wqs_ref[:, h * LANES:(h + 1) * LANES] = jnp.concatenate([x2, x1, zeros], axis=1).astype(BF16)
        t = h * (QK_NOPE + V_HEAD)
        wk_ref[:, h * QK_NOPE:(h + 1) * QK_NOPE] = wukv_ref[:, t:t + QK_NOPE].astype(BF16)
        wv_ref[:, h * V_HEAD:(h + 1) * V_HEAD] = wukv_ref[:, t + QK_NOPE:t + QK_NOPE + V_HEAD].astype(BF16)


def _layout_mla(w_uq, w_ukv):
    full = lambda a: pl.BlockSpec(a, lambda i: (0, 0))
    shapes = [(Q_LORA, MLA_HEADS * HEAD_PAD), (Q_LORA, MLA_HEADS * LANES),
              (KV_LORA, MLA_HEADS * QK_NOPE), (KV_LORA, MLA_HEADS * V_HEAD)]
    return pl.pallas_call(
        _layout_mla_kernel,
        grid=(1,),
        in_specs=[full(w_uq.shape), full(w_ukv.shape)],
        out_specs=[full(s) for s in shapes],
        out_shape=[jax.ShapeDtypeStruct(s, BF16) for s in shapes],
        compiler_params=_params(("arbitrary",)),
        name="layout_mla",
    )(w_uq, w_ukv)


def kernel(x, c, positions, w_ada, b_ada, pre_norm1_g, w_in, gm_ln_g, gm_ln_b, gm_w_s, gm_b_s, w_branch_a, q_norm_g, w_uq, kv_norm_g, w_ukv, w_branch_b, w_out, post_norm1_g, pre_norm2_g, w_up, conv_w, conv_b, w_down, post_norm2_g):
    bsz, seq, d = x.shape
    t = bsz * seq
    x2d = x.reshape(t, d)
    row = lambda a: a.reshape(1, -1)

    cos_t, sin_t = _rope_tables(positions)

    w_zt = _layout_w_in(w_in.T)
    wq, wqs, wk, wv = _layout_mla(w_uq, w_ukv)
    b_s_bcast = jnp.broadcast_to(gm_b_s[:, :, None], (GM_GROUPS, CHUNK, CHUNK))

    mod = _ada(c, w_ada, b_ada).reshape(bsz, N_MOD, d)
    z = _inproj(x2d, mod, row(pre_norm1_g), w_zt, tm=1024)
    yag = _gmlp(z, row(gm_ln_g), row(gm_ln_b), gm_w_s, b_s_bcast, w_branch_a, tm=512)
    q, k, v = _mla_proj(z, cos_t, sin_t, row(q_norm_g), row(kv_norm_g), wq, wqs, wk, wv, tm=512)
    o = _attention(q, k, v, bsz, tq=512, heads=4)
    x1 = _merge(o, yag, z, x2d, mod, row(post_norm1_g), w_branch_b.astype(BF16), w_out.astype(BF16), tm=256)
    act = _ffn_up(x1, mod, row(pre_norm2_g), w_up, conv_w, row(conv_b), tm=1024, tn=512)
    x2 = _ffn_down(act, x1, mod, row(post_norm2_g), w_down.astype(BF16), tm=512)
    return x2.reshape(bsz, seq, d)
```
